```python
import jax, jax.numpy as jnp
from jax import lax
import numpy as np

D_MODEL = 1024
BATCH = 4
SEQ = 4096
DEPTH = 4
DEC_BATCH = 32
DEC_SEQ = 4
PAST_LEN = 8192
PAGE_SIZE = 128

D_MIX = D_MODEL
A_HEAD_DIM = 64
A_HEADS = (D_MIX // 2) // A_HEAD_DIM
A_WIDTH = A_HEADS * A_HEAD_DIM
Q_BLOCK = 128
B_WIDTH = D_MIX // 4
B_BLOCKS = 4
B_BLK = B_WIDTH // B_BLOCKS
LRU_C = 8.0
C_HEAD_DIM = 64
C_HEADS = (D_MIX // 4) // C_HEAD_DIM
C_WIDTH = C_HEADS * C_HEAD_DIM
GDN_CHUNK = 64
CONV_W = 4
P_HEADS = 8
P_DQ = 256
P_DK = P_DQ // 2
N_KEYS = 128
N_EXPERTS = N_KEYS * N_KEYS
P_TOPK = 16
P_BLOCK = 128
EPS = 1e-6

OFF_AQ = 0
OFF_AK = OFF_AQ + A_WIDTH
OFF_AV = OFF_AK + A_WIDTH
OFF_AF = OFF_AV + A_WIDTH
OFF_BX = OFF_AF + A_HEADS
OFF_BG = OFF_BX + B_WIDTH
OFF_CQKV = OFF_BG + B_WIDTH
OFF_CA = OFF_CQKV + 3 * C_WIDTH
OFF_CB = OFF_CA + C_HEADS
OFF_CG = OFF_CB + C_HEADS
D_IN = OFF_CG + C_WIDTH

kernel_name = 'hybrid_fox_rglru_gdn_peer_step'


def rmsnorm(x, g):
    xf = x.astype(jnp.float32)
    y = xf * lax.rsqrt(jnp.mean(xf * xf, axis=-1, keepdims=True) + EPS)
    return (y * g.astype(jnp.float32)).astype(x.dtype)


def l2norm(x):
    xf = x.astype(jnp.float32)
    return xf * lax.rsqrt(jnp.sum(xf * xf, axis=-1, keepdims=True) + EPS)


def causal_dwconv(x, buf, w):
    T = x.shape[1]
    xp = jnp.concatenate([buf.astype(x.dtype), x], axis=1)
    y = sum(xp[:, i:i + T] * w[i] for i in range(CONV_W))
    return y, xp[:, -(CONV_W - 1):]


def gather_pages(pool, page_table):
    g = pool[page_table]
    return g.reshape((g.shape[0], g.shape[1] * g.shape[2]) + g.shape[3:])


def fox_attention(q, k, v, logf):
    B, Tq, H, Dh = q.shape
    Tk = k.shape[1]
    off = Tk - Tq
    c_k = jnp.transpose(lax.cumsum(logf.astype(jnp.float32), axis=1), (0, 2, 1))
    c_q = c_k[:, :, off:]
    qb = min(Q_BLOCK, Tq)
    nb = -(-Tq // qb)
    pad = nb * qb - Tq
    qp = jnp.pad(q, ((0, 0), (0, pad), (0, 0), (0, 0)))
    cqp = jnp.pad(c_q, ((0, 0), (0, 0), (0, pad)))
    q_blocks = qp.reshape(B, nb, qb, H, Dh).transpose(1, 0, 2, 3, 4)
    c_blocks = cqp.reshape(B, H, nb, qb).transpose(2, 0, 1, 3)
    starts = off + jnp.arange(nb, dtype=jnp.int32) * qb
    key_pos = jnp.arange(Tk, dtype=jnp.int32)
    scale = Dh ** -0.5

    def one_block(args):
        qblk, cblk, s0 = args
        qpos = s0 + jnp.arange(qb, dtype=jnp.int32)
        logits = (jnp.einsum('bqhd,bkhd->bhqk', qblk, k).astype(jnp.float32) * scale
                  + (cblk[..., :, None] - c_k[:, :, None, :]))
        logits = jnp.where(key_pos[None, :] <= qpos[:, None], logits, -jnp.inf)
        p = jax.nn.softmax(logits, axis=-1).astype(v.dtype)
        return jnp.einsum('bhqk,bkhd->bqhd', p, v)

    out = lax.map(one_block, (q_blocks, c_blocks, starts))
    return out.transpose(1, 0, 2, 3, 4).reshape(B, nb * qb, H, Dh)[:, :Tq]


def rg_lru(xc, h0, wa, ba, wx, bx, lam):
    B, T, W = xc.shape
    xf = xc.astype(jnp.float32)
    xb = xf.reshape(B, T, B_BLOCKS, B_BLK)
    r = jax.nn.sigmoid(jnp.einsum('btnc,ncd->btnd', xb, wa.astype(jnp.float32)).reshape(B, T, W) + ba)
    i = jax.nn.sigmoid(jnp.einsum('btnc,ncd->btnd', xb, wx.astype(jnp.float32)).reshape(B, T, W) + bx)
    log_a = -LRU_C * r * jax.nn.softplus(-lam.astype(jnp.float32))
    a = jnp.exp(log_a)
    b = jnp.sqrt(-jnp.expm1(2.0 * log_a)) * (i * xf)

    def step(h, ab):
        a_t, b_t = ab
        h = a_t * h + b_t
        return h, h

    hT, hs = lax.scan(step, h0.astype(jnp.float32), (a.transpose(1, 0, 2), b.transpose(1, 0, 2)))
    return hs.transpose(1, 0, 2).astype(xc.dtype), hT.astype(h0.dtype)


def gated_delta_rule(q, k, v, g, beta, S0):
    B, T, H, Dk = q.shape
    Dv = v.shape[-1]
    C = GDN_CHUNK
    n = -(-T // C)
    pad = n * C - T

    def chunks(a):
        a = jnp.pad(a.astype(jnp.float32), ((0, 0), (0, pad)) + ((0, 0),) * (a.ndim - 2))
        a = a.reshape((B, n, C) + a.shape[2:])
        return a.transpose((1, 0, 3, 2) + tuple(range(4, a.ndim)))

    qc = chunks(q) * (Dk ** -0.5)
    kc = chunks(k)
    vc = chunks(v)
    gc = lax.cumsum(chunks(g), axis=3)
    bc = chunks(beta)
    incl = jnp.tril(jnp.ones((C, C), bool))
    strict = jnp.tril(jnp.ones((C, C), bool), -1)
    diff = gc[..., :, None] - gc[..., None, :]
    decay = jnp.where(incl, jnp.exp(jnp.where(incl, diff, 0.0)), 0.0)
    kb = kc * bc[..., None]
    L = jnp.where(strict, jnp.einsum('nbhid,nbhjd->nbhij', kb, kc) * decay, 0.0)
    M = L + jnp.eye(C, dtype=jnp.float32)
    rhs = jnp.concatenate([vc * bc[..., None], kb * jnp.exp(gc)[..., None]], axis=-1)
    sol = lax.linalg.triangular_solve(M, rhs, left_side=True, lower=True, unit_diagonal=True)
    U, Wk = sol[..., :Dv], sol[..., Dv:]
    Aqk = jnp.einsum('nbhid,nbhjd->nbhij', qc, kc) * decay

    def step(S, xs):
        q_i, k_i, U_i, W_i, A_i, g_i = xs
        v_new = U_i - jnp.einsum('bhcd,bhde->bhce', W_i, S)
        o = (jnp.einsum('bhcd,bhde->bhce', q_i * jnp.exp(g_i)[..., None], S)
             + jnp.einsum('bhij,bhje->bhie', A_i, v_new))
        g_last = g_i[..., -1]
        S = (S * jnp.exp(g_last)[..., None, None]
             + jnp.einsum('bhcd,bhce->bhde', k_i * jnp.exp(g_last[..., None] - g_i)[..., None], v_new))
        return S, o

    S_T, o = lax.scan(step, S0.astype(jnp.float32), (qc, kc, U, Wk, Aqk, gc))
    o = o.transpose(1, 0, 3, 2, 4).reshape(B, n * C, H, Dv)[:, :T]
    return o, S_T.astype(S0.dtype)


def gdn_mixer(qkv, a, b, gate, buf, S0, conv_w, A_log, dt_bias, norm_g):
    B, T, _ = qkv.shape
    y, new_buf = causal_dwconv(qkv, buf, conv_w)
    y = jax.nn.silu(y)
    q = l2norm(y[..., :C_WIDTH].reshape(B, T, C_HEADS, C_HEAD_DIM))
    k = l2norm(y[..., C_WIDTH:2 * C_WIDTH].reshape(B, T, C_HEADS, C_HEAD_DIM))
    v = y[..., 2 * C_WIDTH:].reshape(B, T, C_HEADS, C_HEAD_DIM)
    g = -jnp.exp(A_log.astype(jnp.float32)) * jax.nn.softplus(a.astype(jnp.float32) + dt_bias)
    beta = jax.nn.sigmoid(b.astype(jnp.float32))
    o, S_T = gated_delta_rule(q, k, v, g, beta, S0)
    o = rmsnorm(o, norm_g).reshape(B, T, C_WIDTH) * jax.nn.silu(gate.astype(jnp.float32))
    return o.astype(qkv.dtype), S_T, new_buf


def peer_ffn(x, wq, keys, u_tab, v_tab):
    N, D = x.shape
    q = (x @ wq).reshape(N, P_HEADS, 2, P_DK).astype(jnp.float32)
    s = jnp.einsum('nhpd,hpkd->nhpk', q, keys.astype(jnp.float32))
    s_top, i_top = lax.top_k(s, P_TOPK)
    cand = (s_top[:, :, 0, :, None] + s_top[:, :, 1, None, :]).reshape(N, P_HEADS, P_TOPK * P_TOPK)
    cand_idx = (i_top[:, :, 0, :, None] * N_KEYS + i_top[:, :, 1, None, :]).reshape(N, P_HEADS, P_TOPK * P_TOPK)
    best, pos = lax.top_k(cand, P_TOPK)
    eidx = jnp.take_along_axis(cand_idx, pos, axis=-1)
    gate = jax.nn.softmax(best, axis=-1)
    nb = -(-N // P_BLOCK)
    pad = nb * P_BLOCK - N
    xp = jnp.pad(x, ((0, pad), (0, 0))).reshape(nb, P_BLOCK, D)
    ep = jnp.pad(eidx, ((0, pad), (0, 0), (0, 0))).reshape(nb, P_BLOCK, P_HEADS, P_TOPK)
    gp = jnp.pad(gate, ((0, pad), (0, 0), (0, 0))).reshape(nb, P_BLOCK, P_HEADS, P_TOPK)

    def one_block(args):
        xb, eb, gb = args
        u = jnp.take(u_tab, eb, axis=0)
        hid = jax.nn.gelu(jnp.einsum('bhkd,bd->bhk', u, xb).astype(jnp.float32), approximate=False) * gb
        vv = jnp.take(v_tab, eb, axis=0)
        return jnp.einsum('bhk,bhkd->bd', hid.astype(x.dtype), vv)

    out = lax.map(one_block, (xp, ep, gp))
    return out.reshape(nb * P_BLOCK, D)[:N]


def layer_forward(h, past, lru_h0, lru_buf0, gdn_S0, gdn_buf0, lw):
    B, T, D = h.shape
    z = rmsnorm(h, lw['norm1_g']) @ lw['w_in']
    qa = z[..., OFF_AQ:OFF_AK].reshape(B, T, A_HEADS, A_HEAD_DIM)
    ka = z[..., OFF_AK:OFF_AV].reshape(B, T, A_HEADS, A_HEAD_DIM)
    va = z[..., OFF_AV:OFF_AF].reshape(B, T, A_HEADS, A_HEAD_DIM)
    logf = jax.nn.log_sigmoid(z[..., OFF_AF:OFF_BX].astype(jnp.float32) + lw['b_f'])
    if past is None:
        k_all, v_all, lf_all = ka, va, logf
    else:
        pk, pv, pl = past
        k_all = jnp.concatenate([pk.astype(ka.dtype), ka], axis=1)
        v_all = jnp.concatenate([pv.astype(va.dtype), va], axis=1)
        lf_all = jnp.concatenate([pl.astype(jnp.float32), logf], axis=1)
    ya = fox_attention(qa, k_all, v_all, lf_all).reshape(B, T, A_WIDTH)
    xb, new_lru_buf = causal_dwconv(z[..., OFF_BX:OFF_BG], lru_buf0, lw['lru_conv_w'])
    xb = xb + lw['lru_conv_b']
    hb, lru_hT = rg_lru(xb, lru_h0, lw['lru_wa'], lw['lru_ba'], lw['lru_wx'], lw['lru_bx'], lw['lru_lambda'])
    yb = (hb * jax.nn.gelu(z[..., OFF_BG:OFF_CQKV])).astype(h.dtype)
    yc, gdn_ST, new_gdn_buf = gdn_mixer(z[..., OFF_CQKV:OFF_CA], z[..., OFF_CA:OFF_CB], z[..., OFF_CB:OFF_CG],
                                        z[..., OFF_CG:D_IN], gdn_buf0, gdn_S0, lw['gdn_conv_w'],
                                        lw['gdn_A_log'], lw['gdn_dt_bias'], lw['gdn_norm_g'])
    h = h + jnp.concatenate([ya.astype(h.dtype), yb, yc.astype(h.dtype)], axis=-1) @ lw['w_out']
    h2 = rmsnorm(h, lw['norm2_g']).reshape(B * T, D)
    h = h + peer_ffn(h2, lw['peer_wq'], lw['peer_keys'], lw['peer_u'], lw['peer_v']).reshape(B, T, D)
    return h, (ka, va, logf, lru_hT, new_lru_buf, gdn_ST, new_gdn_buf)


def trunk(x, paged, lru_h, lru_buf, gdn_S, gdn_buf, W, final_g):
    h = x
    per_layer = []
    for l in range(DEPTH):
        lw = {name: arr[l] for name, arr in W.items()}
        past = None
        if paged is not None:
            ck, cv, cl, pt = paged
            past = (gather_pages(ck[l], pt), gather_pages(cv[l], pt), gather_pages(cl[l], pt))
        h, st = layer_forward(h, past, lru_h[l], lru_buf[l], gdn_S[l], gdn_buf[l], lw)
        per_layer.append(st)
    new_state = tuple(jnp.stack([st[i] for st in per_layer], axis=0) for i in range(len(per_layer[0])))
    return rmsnorm(h, final_g), new_state


def setup_inputs(seed: int = 0) -> dict:
    key = jax.random.key(seed)
    ks = iter(jax.random.split(key, 48))
    f32 = jnp.float32

    def nrm(shape, scale):
        return jax.random.normal(next(ks), shape, f32) * scale

    def unif(shape, lo, hi):
        return jax.random.uniform(next(ks), shape, f32, minval=lo, maxval=hi)

    n_pages = PAST_LEN // PAGE_SIZE
    n_used = DEC_BATCH * n_pages
    n_pool = n_used + n_used // 4
    x_prompt = nrm((BATCH, SEQ, D_MODEL), 1.0)
    x_sample = nrm((DEC_BATCH, DEC_SEQ, D_MODEL), 1.0)
    cache_k = nrm((DEPTH, n_pool, PAGE_SIZE, A_HEADS, A_HEAD_DIM), 1.0)
    cache_v = nrm((DEPTH, n_pool, PAGE_SIZE, A_HEADS, A_HEAD_DIM), 1.0)
    cache_logf = jax.nn.log_sigmoid(2.0 + nrm((DEPTH, n_pool, PAGE_SIZE, A_HEADS), 1.0))
    page_table = jax.random.permutation(next(ks), n_pool)[:n_used].reshape(DEC_BATCH, n_pages).astype(jnp.int32)
    state_lru_h = nrm((DEPTH, DEC_BATCH, B_WIDTH), 0.5)
    state_lru_conv = nrm((DEPTH, DEC_BATCH, CONV_W - 1, B_WIDTH), 1.0)
    state_gdn_S = nrm((DEPTH, DEC_BATCH, C_HEADS, C_HEAD_DIM, C_HEAD_DIM), 0.1)
    state_gdn_conv = nrm((DEPTH, DEC_BATCH, CONV_W - 1, 3 * C_WIDTH), 1.0)
    norm1_g = 1.0 + nrm((DEPTH, D_MODEL), 0.02)
    w_in = nrm((DEPTH, D_MODEL, D_IN), D_MODEL ** -0.5)
    b_f = 2.0 + nrm((DEPTH, A_HEADS), 0.1)
    lru_conv_w = nrm((DEPTH, CONV_W, B_WIDTH), 0.5)
    lru_conv_b = nrm((DEPTH, B_WIDTH), 0.01)
    lru_wa = nrm((DEPTH, B_BLOCKS, B_BLK, B_BLK), B_BLK ** -0.5)
    lru_ba = nrm((DEPTH, B_WIDTH), 0.01)
    lru_wx = nrm((DEPTH, B_BLOCKS, B_BLK, B_BLK), B_BLK ** -0.5)
    lru_bx = nrm((DEPTH, B_WIDTH), 0.01)
    a0 = unif((DEPTH, B_WIDTH), 0.9, 0.999) ** (1.0 / LRU_C)
    lru_lambda = jnp.log(a0) - jnp.log1p(-a0)
    gdn_conv_w = nrm((DEPTH, CONV_W, 3 * C_WIDTH), 0.5)
    gdn_A_log = jnp.log(unif((DEPTH, C_HEADS), 1.0, 16.0))
    dt = jnp.exp(unif((DEPTH, C_HEADS), float(np.log(1e-3)), float(np.log(1e-1))))
    gdn_dt_bias = dt + jnp.log(-jnp.expm1(-dt))
    gdn_norm_g = 1.0 + nrm((DEPTH, C_HEAD_DIM), 0.02)
    w_out = nrm((DEPTH, D_MIX, D_MODEL), 0.5 * D_MIX ** -0.5)
    norm2_g = 1.0 + nrm((DEPTH, D_MODEL), 0.02)
    peer_wq = nrm((DEPTH, D_MODEL, P_HEADS * P_DQ), D_MODEL ** -0.5)
    peer_keys = nrm((DEPTH, P_HEADS, 2, N_KEYS, P_DK), P_DK ** -0.5)
    peer_u = nrm((DEPTH, N_EXPERTS, D_MODEL), D_MODEL ** -0.5)
    peer_v = nrm((DEPTH, N_EXPERTS, D_MODEL), 0.2)
    final_norm_g = 1.0 + nrm((D_MODEL,), 0.02)
    return {'x_prompt': x_prompt, 'x_sample': x_sample, 'cache_k': cache_k, 'cache_v': cache_v,
            'cache_logf': cache_logf, 'page_table': page_table, 'state_lru_h': state_lru_h,
            'state_lru_conv': state_lru_conv, 'state_gdn_S': state_gdn_S, 'state_gdn_conv': state_gdn_conv,
            'norm1_g': norm1_g, 'w_in': w_in, 'b_f': b_f, 'lru_conv_w': lru_conv_w, 'lru_conv_b': lru_conv_b,
            'lru_wa': lru_wa, 'lru_ba': lru_ba, 'lru_wx': lru_wx, 'lru_bx': lru_bx, 'lru_lambda': lru_lambda,
            'gdn_conv_w': gdn_conv_w, 'gdn_A_log': gdn_A_log, 'gdn_dt_bias': gdn_dt_bias, 'gdn_norm_g': gdn_norm_g,
            'w_out': w_out, 'norm2_g': norm2_g, 'peer_wq': peer_wq, 'peer_keys': peer_keys,
            'peer_u': peer_u, 'peer_v': peer_v, 'final_norm_g': final_norm_g}


def reference(x_prompt, x_sample, cache_k, cache_v, cache_logf, page_table, state_lru_h, state_lru_conv,
              state_gdn_S, state_gdn_conv, norm1_g, w_in, b_f, lru_conv_w, lru_conv_b, lru_wa, lru_ba,
              lru_wx, lru_bx, lru_lambda, gdn_conv_w, gdn_A_log, gdn_dt_bias, gdn_norm_g, w_out, norm2_g,
              peer_wq, peer_keys, peer_u, peer_v, final_norm_g):
    W = {'norm1_g': norm1_g, 'w_in': w_in, 'b_f': b_f, 'lru_conv_w': lru_conv_w, 'lru_conv_b': lru_conv_b,
         'lru_wa': lru_wa, 'lru_ba': lru_ba, 'lru_wx': lru_wx, 'lru_bx': lru_bx, 'lru_lambda': lru_lambda,
         'gdn_conv_w': gdn_conv_w, 'gdn_A_log': gdn_A_log, 'gdn_dt_bias': gdn_dt_bias, 'gdn_norm_g': gdn_norm_g,
         'w_out': w_out, 'norm2_g': norm2_g, 'peer_wq': peer_wq, 'peer_keys': peer_keys,
         'peer_u': peer_u, 'peer_v': peer_v}
    bp = x_prompt.shape[0]
    h0_p = jnp.zeros((DEPTH, bp, B_WIDTH), state_lru_h.dtype)
    lbuf_p = jnp.zeros((DEPTH, bp, CONV_W - 1, B_WIDTH), x_prompt.dtype)
    S0_p = jnp.zeros((DEPTH, bp, C_HEADS, C_HEAD_DIM, C_HEAD_DIM), state_gdn_S.dtype)
    gbuf_p = jnp.zeros((DEPTH, bp, CONV_W - 1, 3 * C_WIDTH), x_prompt.dtype)
    y_prompt, (k_prompt, v_prompt, logf_prompt, lru_h_prompt, lru_conv_prompt, gdn_S_prompt,
               gdn_conv_prompt) = trunk(x_prompt, None, h0_p, lbuf_p, S0_p, gbuf_p, W, final_norm_g)
    y_sample, (k_sample, v_sample, logf_sample, lru_h_sample, lru_conv_sample, gdn_S_sample,
               gdn_conv_sample) = trunk(x_sample, (cache_k, cache_v, cache_logf, page_table), state_lru_h,
                                        state_lru_conv, state_gdn_S, state_gdn_conv, W, final_norm_g)
    return (y_prompt, y_sample, k_prompt, v_prompt, logf_prompt, k_sample, v_sample, logf_sample,
            lru_h_prompt, lru_h_sample, lru_conv_prompt, lru_conv_sample, gdn_S_prompt, gdn_S_sample,
            gdn_conv_prompt, gdn_conv_sample)
```

```python
import functools

import jax
import jax.numpy as jnp
from jax import lax
from jax.experimental import pallas as pl
from jax.experimental.pallas import tpu as pltpu

F32 = jnp.float32
BF16 = jnp.bfloat16
I32 = jnp.int32
HI = lax.Precision.HIGHEST

D_MODEL = 1024
A_HEADS = 8
A_HEAD_DIM = 64
A_WIDTH = A_HEADS * A_HEAD_DIM
B_WIDTH = 256
B_BLOCKS = 4
LRU_C = 8.0
C_HEADS = 4
C_HEAD_DIM = 64
C_WIDTH = C_HEADS * C_HEAD_DIM
GDN_CHUNK = 64
CONV_W = 4
P_HEADS = 8
P_DK = 128
N_KEYS = 128
N_EXPERTS = N_KEYS * N_KEYS
P_TOPK = 16
PAGE_SIZE = 128
EPS = 1e-6
SAMPLE_ROWS = 8
N_SMALL = 16

OFF_AQ = 0
OFF_AF = OFF_AQ + 3 * A_WIDTH
OFF_BX = OFF_AF + A_HEADS
OFF_CQKV = OFF_BX + 2 * B_WIDTH
OFF_CA = OFF_CQKV + 3 * C_WIDTH
OFF_CG = OFF_CA + 2 * C_HEADS
D_IN = OFF_CG + C_WIDTH

VMEM_LIMIT = 48 * 1024 * 1024


def _params(sem, vmem=VMEM_LIMIT):
    return pltpu.CompilerParams(dimension_semantics=sem, vmem_limit_bytes=vmem)


def _dot(a, b, precision=None):
    return jnp.dot(a, b, precision=precision, preferred_element_type=F32)


def _dot_nt(a, b, precision=None):
    return lax.dot_general(a, b, (((1,), (1,)), ((), ())), precision=precision, preferred_element_type=F32)


def _dot_tn(a, b, precision=None):
    return lax.dot_general(a, b, (((0,), (0,)), ((), ())), precision=precision, preferred_element_type=F32)


def _rms(x, g):
    return x * lax.rsqrt(jnp.mean(x * x, axis=-1, keepdims=True) + EPS) * g


def _gelu_erf(x):
    return 0.5 * x * (1.0 + lax.erf(x * 0.7071067811865476))


def _iota(shape, dim):
    return lax.broadcasted_iota(I32, shape, dim)


def _inproj_kernel(x_ref, g_ref, wa_ref, wb_ref, wc_ref, ws_ref,
                   q_ref, k_ref, v_ref, kb_ref, vb_ref, zb_ref, zc_ref, st_ref):
    hn = _rms(x_ref[...], g_ref[...])
    hb = hn.astype(BF16)
    za = _dot(hb, wa_ref[...])
    q_ref[...] = (za[:, :A_WIDTH] * (A_HEAD_DIM ** -0.5)).astype(BF16)
    k = za[:, A_WIDTH:2 * A_WIDTH]
    v = za[:, 2 * A_WIDTH:]
    k_ref[...] = k
    v_ref[...] = v
    kb_ref[...] = k.astype(BF16)
    vb_ref[...] = v.astype(BF16)
    zb_ref[...] = _dot(hb, wb_ref[...])
    zc_ref[...] = _dot(hb, wc_ref[...])
    st_ref[...] = _dot_nt(ws_ref[...], hn, HI)


def in_proj(h, g, wa, wb, wc, ws, tm):
    n = h.shape[0]
    grid = (n // tm,)
    row = lambda w: pl.BlockSpec((tm, w), lambda i: (i, 0))
    full = lambda a: pl.BlockSpec(a.shape, lambda i: (0,) * a.ndim)
    out_shape = [
        jax.ShapeDtypeStruct((n, A_WIDTH), BF16),
        jax.ShapeDtypeStruct((n, A_WIDTH), F32),
        jax.ShapeDtypeStruct((n, A_WIDTH), F32),
        jax.ShapeDtypeStruct((n, A_WIDTH), BF16),
        jax.ShapeDtypeStruct((n, A_WIDTH), BF16),
        jax.ShapeDtypeStruct((n, 2 * B_WIDTH), F32),
        jax.ShapeDtypeStruct((n, 4 * C_WIDTH), F32),
        jax.ShapeDtypeStruct((N_SMALL, n), F32),
    ]
    out_specs = [row(A_WIDTH)] * 5 + [row(2 * B_WIDTH), row(4 * C_WIDTH),
                                      pl.BlockSpec((N_SMALL, tm), lambda i: (0, i))]
    return pl.pallas_call(
        _inproj_kernel, grid=grid, out_shape=out_shape,
        in_specs=[row(D_MODEL), full(g), full(wa), full(wb), full(wc), full(ws)],
        out_specs=out_specs, compiler_params=_params(("parallel",)), name="in_proj",
    )(h, g, wa, wb, wc, ws)


CUMSUM_CHUNK = 512


def _logf_kernel(st_ref, bf_ref, lf_ref, c_ref, *, cumsum):
    lf = jax.nn.log_sigmoid(st_ref[0:A_HEADS, :] + bf_ref[...])
    lf_ref[...] = lf
    if not cumsum:
        c_ref[...] = lf
        return
    t = lf.shape[1]
    ch = min(CUMSUM_CHUNK, t)
    tri = (_iota((ch, ch), 0) <= _iota((ch, ch), 1)).astype(F32)
    carry = jnp.zeros((A_HEADS, 1), F32)
    for j in range(t // ch):
        cs = _dot(lf[:, j * ch:(j + 1) * ch], tri, HI) + carry
        c_ref[:, j * ch:(j + 1) * ch] = cs
        carry = cs[:, ch - 1:ch]


def logf_rows(st, bf, nb, cumsum):
    n = st.shape[1]
    t = n // nb
    spec_in = pl.BlockSpec((N_SMALL, t), lambda b: (0, b))
    spec_out = pl.BlockSpec((A_HEADS, t), lambda b: (0, b))
    return pl.pallas_call(
        functools.partial(_logf_kernel, cumsum=cumsum), grid=(nb,),
        out_shape=[jax.ShapeDtypeStruct((A_HEADS, n), F32)] * 2,
        in_specs=[spec_in, pl.BlockSpec(bf.shape, lambda b: (0, 0))],
        out_specs=[spec_out, spec_out], compiler_params=_params(("parallel",)), name="logf",
    )(st, bf)


def _attn_prompt_kernel(q_ref, k_ref, v_ref, c_ref, o_ref, m_ref, l_ref, acc_ref):
    qi = pl.program_id(2)
    ki = pl.program_id(3)
    tq, tk = q_ref.shape[0], k_ref.shape[0]
    lane = _iota((1, 2 * A_HEAD_DIM), 1)

    @pl.when(ki == 0)
    def _():
        m_ref[...] = jnp.full(m_ref.shape, -jnp.inf, F32)
        l_ref[...] = jnp.zeros(l_ref.shape, F32)
        acc_ref[...] = jnp.zeros(acc_ref.shape, F32)

    @pl.when(ki <= qi)
    def _():
        q = q_ref[...]
        k = k_ref[...]
        v = v_ref[...]
        visible = (_iota((tq, tk), 1) <= _iota((tq, tk), 0)) | (ki < qi)
        for h in range(2):
            head = (lane >= A_HEAD_DIM) if h else (lane < A_HEAD_DIM)
            s = _dot_nt(jnp.where(head, q, jnp.zeros_like(q)), k) - c_ref[0, h:h + 1, :]
            s = jnp.where(visible, s, -jnp.inf)
            m_prev = m_ref[h]
            m_new = jnp.maximum(m_prev, jnp.max(s, axis=1, keepdims=True))
            alpha = jnp.exp(m_prev - m_new)
            p = jnp.exp(s - m_new)
            l_ref[h] = alpha * l_ref[h] + jnp.sum(p, axis=1, keepdims=True)
            acc_ref[h] = alpha * acc_ref[h] + _dot(p.astype(BF16), v)
            m_ref[h] = m_new

    @pl.when(ki == qi)
    def _():
        o0 = acc_ref[0] / l_ref[0]
        o1 = acc_ref[1] / l_ref[1]
        o_ref[...] = jnp.where(lane < A_HEAD_DIM, o0, o1).astype(o_ref.dtype)


def attn_prompt(q, kb, vb, c, nb, tq):
    n = q.shape[0]
    t = n // nb
    nq = t // tq
    w = 2 * A_HEAD_DIM
    qspec = pl.BlockSpec((tq, w), lambda b, p, qi, ki: (b * nq + qi, p))
    kspec = pl.BlockSpec((tq, w), lambda b, p, qi, ki: (b * nq + jnp.minimum(ki, qi), p))
    cspec = pl.BlockSpec((1, 2, tq), lambda b, p, qi, ki: (p, 0, b * nq + jnp.minimum(ki, qi)))
    return pl.pallas_call(
        _attn_prompt_kernel, grid=(nb, A_HEADS // 2, nq, nq),
        out_shape=jax.ShapeDtypeStruct((n, A_WIDTH), BF16),
        in_specs=[qspec, kspec, kspec, cspec], out_specs=qspec,
        scratch_shapes=[pltpu.VMEM((2, tq, 1), F32), pltpu.VMEM((2, tq, 1), F32), pltpu.VMEM((2, tq, w), F32)],
        compiler_params=_params(("parallel", "parallel", "parallel", "arbitrary")), name="attn_prompt",
    )(q, kb, vb, c)


def _attn_sample_kernel(pt_ref, q_ref, ck_ref, cv_ref, cl_ref, kn_ref, vn_ref, ln_ref, o_ref,
                        qbd_ref, m_ref, l_ref, acc_ref, carry_ref, *, n_pages, t_new):
    del pt_ref
    j = pl.program_id(1)
    nr = t_new * A_HEADS
    lane_head = _iota((A_HEADS, A_WIDTH), 1) // A_HEAD_DIM
    sub = _iota((A_HEADS, A_WIDTH), 0)
    own = lane_head == sub

    @pl.when(j == 0)
    def _():
        q = q_ref[...].astype(F32)
        parts = [jnp.where(own, jnp.broadcast_to(q[t:t + 1, :], (A_HEADS, A_WIDTH)), 0.0) for t in range(t_new)]
        qbd_ref[...] = jnp.concatenate(parts, axis=0).astype(BF16)
        m_ref[...] = jnp.full(m_ref.shape, -jnp.inf, F32)
        l_ref[...] = jnp.zeros(l_ref.shape, F32)
        acc_ref[...] = jnp.zeros(acc_ref.shape, F32)
        carry_ref[...] = jnp.zeros(carry_ref.shape, F32)

    tri = (_iota((PAGE_SIZE, PAGE_SIZE), 0) <= _iota((PAGE_SIZE, PAGE_SIZE), 1)).astype(F32)

    def page(k, v, lf, visible):
        c = _dot(lf, tri, HI) + carry_ref[...]
        carry_ref[...] = c[:, PAGE_SIZE - 1:PAGE_SIZE]
        s = _dot_nt(qbd_ref[...], k.astype(BF16)) - jnp.concatenate([c] * t_new, axis=0)
        if visible is not None:
            s = jnp.where(visible, s, -jnp.inf)
        m_prev = m_ref[...]
        m_new = jnp.maximum(m_prev, jnp.max(s, axis=1, keepdims=True))
        alpha = jnp.exp(m_prev - m_new)
        p = jnp.exp(s - m_new)
        l_ref[...] = alpha * l_ref[...] + jnp.sum(p, axis=1, keepdims=True)
        acc_ref[...] = alpha * acc_ref[...] + _dot(p.astype(BF16), v.astype(BF16))
        m_ref[...] = m_new

    @pl.when(j < n_pages)
    def _():
        page(ck_ref[...], cv_ref[...], cl_ref[...], None)

    @pl.when(j == n_pages)
    def _():
        pad = jnp.zeros((PAGE_SIZE - SAMPLE_ROWS, A_WIDTH), F32)
        k = jnp.concatenate([kn_ref[...], pad], axis=0)
        v = jnp.concatenate([vn_ref[...], pad], axis=0)
        visible = _iota((nr, PAGE_SIZE), 1) <= _iota((nr, PAGE_SIZE), 0) // A_HEADS
        page(k, v, ln_ref[...], visible)
        o = acc_ref[...] / l_ref[...]
        rows = [jnp.sum(jnp.where(own, o[t * A_HEADS:(t + 1) * A_HEADS, :], 0.0), axis=0, keepdims=True)
                for t in range(t_new)]
        rows.append(jnp.zeros((SAMPLE_ROWS - t_new, A_WIDTH), F32))
        o_ref[...] = jnp.concatenate(rows, axis=0).astype(o_ref.dtype)


def attn_sample(page_table, q, cache_k, cache_v, cache_lt, kn, vn, lfn, layer, t_new):
    nb, n_pages = page_table.shape
    nr = t_new * A_HEADS
    last = n_pages - 1
    seq = lambda w: pl.BlockSpec((SAMPLE_ROWS, w), lambda b, j, pt: (b, 0))
    pg = lambda r, w: pl.BlockSpec((None, None, r, w),
                                   lambda b, j, pt: (layer, pt[b, jnp.minimum(j, last)], 0, 0))
    grid_spec = pltpu.PrefetchScalarGridSpec(
        num_scalar_prefetch=1, grid=(nb, n_pages + 1),
        in_specs=[seq(A_WIDTH), pg(PAGE_SIZE, A_WIDTH), pg(PAGE_SIZE, A_WIDTH), pg(A_HEADS, PAGE_SIZE),
                  seq(A_WIDTH), seq(A_WIDTH), pl.BlockSpec((None, A_HEADS, PAGE_SIZE), lambda b, j, pt: (b, 0, 0))],
        out_specs=seq(A_WIDTH),
        scratch_shapes=[pltpu.VMEM((nr, A_WIDTH), BF16), pltpu.VMEM((nr, 1), F32), pltpu.VMEM((nr, 1), F32),
                        pltpu.VMEM((nr, A_WIDTH), F32), pltpu.VMEM((A_HEADS, 1), F32)])
    return pl.pallas_call(
        functools.partial(_attn_sample_kernel, n_pages=n_pages, t_new=t_new), grid_spec=grid_spec,
        out_shape=jax.ShapeDtypeStruct(q.shape, BF16),
        compiler_params=_params(("parallel", "arbitrary")), name="attn_sample",
    )(page_table, q, cache_k, cache_v, cache_lt, kn, vn, lfn)


CONV_PAD = 8


def _conv_block(xp_ref, x, w_ref, tc):
    xp_ref[CONV_PAD:CONV_PAD + tc, :] = x
    y = xp_ref[CONV_PAD - 3:CONV_PAD - 3 + tc, :] * w_ref[0:1, :]
    for i in range(1, CONV_W):
        y = y + xp_ref[CONV_PAD - 3 + i:CONV_PAD - 3 + i + tc, :] * w_ref[i:i + 1, :]
    return y


def _lru_kernel(z_ref, buf_ref, h0_ref, cw_ref, cb_ref, wa_ref, ba_ref, wx_ref, bx_ref, lam_ref,
                y_ref, ht_ref, nb_ref, xp_ref, hc_ref, *, t_valid):
    ti = pl.program_id(1)
    tc = z_ref.shape[0]

    @pl.when(ti == 0)
    def _():
        xp_ref[CONV_PAD - 3:CONV_PAD, :] = buf_ref[...]
        hc_ref[...] = h0_ref[...]

    x = z_ref[:, :B_WIDTH]
    xc = _conv_block(xp_ref, x, cw_ref, tc) + cb_ref[...]
    r = jax.nn.sigmoid(_dot(xc, wa_ref[...], HI) + ba_ref[...])
    ig = jax.nn.sigmoid(_dot(xc, wx_ref[...], HI) + bx_ref[...])
    log_a = -LRU_C * r * jax.nn.softplus(-lam_ref[...])
    a = jnp.exp(log_a)
    b = jnp.sqrt(1.0 - jnp.exp(2.0 * log_a)) * (ig * xc)
    row = _iota((tc, B_WIDTH), 0)
    d = 1
    while d < tc:
        keep = row >= d
        a_prev = jnp.where(keep, pltpu.roll(a, d, 0), 1.0)
        b_prev = jnp.where(keep, pltpu.roll(b, d, 0), 0.0)
        b = a * b_prev + b
        a = a * a_prev
        d *= 2
    hs = a * hc_ref[...] + b
    y_ref[...] = (hs * jax.nn.gelu(z_ref[:, B_WIDTH:])).astype(y_ref.dtype)
    hc_ref[...] = hs[t_valid - 1:t_valid, :]
    xp_ref[CONV_PAD - 3:CONV_PAD, :] = xp_ref[CONV_PAD + t_valid - 3:CONV_PAD + t_valid, :]

    @pl.when(ti == pl.num_programs(1) - 1)
    def _():
        ht_ref[...] = hc_ref[...]
        nb_ref[...] = xp_ref[CONV_PAD - 3:CONV_PAD, :]


def lru_mixer(zb, buf0, h0, cw, cb, wa, ba, wx, bx, lam, nb, tc, t_valid):
    n = zb.shape[0]
    nt = n // nb // tc
    full = lambda a: pl.BlockSpec(a.shape, lambda b, t: (0,) * a.ndim)
    per_b = lambda r: pl.BlockSpec((None, r, B_WIDTH), lambda b, t: (b, 0, 0))
    return pl.pallas_call(
        functools.partial(_lru_kernel, t_valid=t_valid), grid=(nb, nt),
        out_shape=[jax.ShapeDtypeStruct((n, B_WIDTH), BF16), jax.ShapeDtypeStruct((nb, 1, B_WIDTH), F32),
                   jax.ShapeDtypeStruct((nb, CONV_W - 1, B_WIDTH), F32)],
        in_specs=[pl.BlockSpec((tc, 2 * B_WIDTH), lambda b, t: (b * nt + t, 0)), per_b(CONV_W - 1), per_b(1),
                  full(cw), full(cb), full(wa), full(ba), full(wx), full(bx), full(lam)],
        out_specs=[pl.BlockSpec((tc, B_WIDTH), lambda b, t: (b * nt + t, 0)), per_b(1), per_b(CONV_W - 1)],
        scratch_shapes=[pltpu.VMEM((CONV_PAD + tc, B_WIDTH), F32), pltpu.VMEM((1, B_WIDTH), F32)],
        compiler_params=_params(("parallel", "arbitrary")), name="lru",
    )(zb, buf0, h0, cw, cb, wa, ba, wx, bx, lam)


PAIR = 2 * C_HEAD_DIM


def _stack_heads(x, lo_mask):
    return jnp.concatenate([jnp.where(lo_mask, x, 0.0), jnp.where(lo_mask, 0.0, x)], axis=0)


def _row_to_col(row, eye):
    return jnp.sum(jnp.where(eye, row, 0.0), axis=1, keepdims=True)


def _gdn_kernel(z_ref, sc_ref, buf_ref, s0_ref, cw_ref, alog_ref, dt_ref, ng_ref, ones_hd_ref,
                y_ref, st_ref, nb_ref, xp_ref, s_ref, o_ref, ys_ref, *, t_valid, n_chunks):
    ti = pl.program_id(1)
    tc = z_ref.shape[0]
    c = GDN_CHUNK
    qkv_w = 3 * C_WIDTH

    @pl.when(ti == 0)
    def _():
        xp_ref[CONV_PAD - 3:CONV_PAD, :] = buf_ref[...]
        s_ref[...] = s0_ref[...]

    y = jax.nn.silu(_conv_block(xp_ref, z_ref[:, :qkv_w], cw_ref, tc))
    xp_ref[CONV_PAD - 3:CONV_PAD, :] = xp_ref[CONV_PAD + t_valid - 3:CONV_PAD + t_valid, :]
    padded = tc < n_chunks * c
    if padded:
        y = jnp.concatenate([y, jnp.zeros((n_chunks * c - tc, qkv_w), F32)], axis=0)
    ys_ref[...] = y

    r128 = _iota((PAIR, PAIR), 0)
    c128 = _iota((PAIR, PAIR), 1)
    same = (r128 // c) == (c128 // c)
    eye = r128 == c128
    incl = same & (c128 <= r128)
    strict = same & (c128 < r128)
    ones_bd = same.astype(F32)
    tri_bd = (same & (r128 <= c128)).astype(F32)
    last_sel = same & (c128 % c == c - 1)
    lo_lane = _iota((1, PAIR), 1) < c
    lane_t = _iota((1, PAIR), 1) % c
    ident = eye.astype(F32)

    def l2n(x):
        return x * lax.rsqrt(_dot(x * x, ones_bd, HI) + EPS)

    def chunk(ci, carry):
        r0 = pl.multiple_of(ci * c, c)
        sc = sc_ref[ci]
        g4 = -jnp.exp(alog_ref[...]) * jax.nn.softplus(sc[8:12, :] + dt_ref[...])
        b4 = jax.nn.sigmoid(sc[12:16, :])
        for p in range(C_HEADS // 2):
            g_row = jnp.concatenate([g4[2 * p:2 * p + 1, :], g4[2 * p + 1:2 * p + 2, :]], axis=1)
            b_row = jnp.concatenate([b4[2 * p:2 * p + 1, :], b4[2 * p + 1:2 * p + 2, :]], axis=1)
            if padded:
                ok = lane_t < t_valid
                g_row = jnp.where(ok, g_row, 0.0)
                b_row = jnp.where(ok, b_row, 0.0)
            gc_row = _dot(jnp.broadcast_to(g_row, (8, PAIR)), tri_bd, HI)[0:1, :]
            gc_col = _row_to_col(gc_row, eye)
            b_col = _row_to_col(b_row, eye)
            gl_col = jnp.sum(jnp.where(last_sel, gc_row, 0.0), axis=1, keepdims=True)
            decay = jnp.where(incl, jnp.exp(jnp.where(incl, gc_col - gc_row, 0.0)), 0.0)

            lanes = slice(p * PAIR, (p + 1) * PAIR)
            rows = pl.ds(r0, c)
            q = l2n(ys_ref[rows, p * PAIR:(p + 1) * PAIR]) * (C_HEAD_DIM ** -0.5)
            k = l2n(ys_ref[rows, C_WIDTH + p * PAIR:C_WIDTH + (p + 1) * PAIR])
            v = ys_ref[rows, 2 * C_WIDTH + p * PAIR:2 * C_WIDTH + (p + 1) * PAIR]
            q_st = _stack_heads(q, lo_lane)
            k_st = _stack_heads(k, lo_lane)
            v_st = _stack_heads(v, lo_lane)
            kb_st = k_st * b_col
            lmat = jnp.where(strict, _dot_nt(kb_st, k_st, HI) * decay, 0.0)
            tinv = ident - lmat
            pw = lmat
            for _ in range(5):
                pw = _dot(pw, pw, HI)
                tinv = tinv + _dot(tinv, pw, HI)
            u = _dot(tinv, v_st * b_col, HI)
            wk = _dot(tinv, kb_st * jnp.exp(gc_col), HI)
            aqk = _dot_nt(q_st, k_st, HI) * decay
            s = s_ref[p]
            v_new = u - _dot(wk, s, HI)
            o_st = _dot(q_st * jnp.exp(gc_col), s, HI) + _dot(aqk, v_new, HI)
            o_ref[pl.ds(r0, c), lanes] = o_st[:c, :] + o_st[c:, :]
            kd = k_st * jnp.exp(gl_col - gc_col)
            s_ref[p] = s * jnp.exp(gl_col) + _dot_tn(kd, v_new, HI)
        return carry

    lax.fori_loop(0, n_chunks, chunk, 0)

    o = o_ref[0:tc, :]
    ms = _dot(o * o, ones_hd_ref[...], HI)
    on = o * lax.rsqrt(ms + EPS) * ng_ref[...]
    y_ref[...] = (on * jax.nn.silu(z_ref[:, qkv_w:])).astype(y_ref.dtype)

    @pl.when(ti == pl.num_programs(1) - 1)
    def _():
        st_ref[...] = s_ref[...]
        nb_ref[...] = xp_ref[CONV_PAD - 3:CONV_PAD, :]


def gdn_mixer(zc, scal, buf0, s0, cw, alog, dt, ng, ones_hd, nb, tc, t_valid, n_chunks):
    n = zc.shape[0]
    nt = n // nb // tc
    qkv_w = 3 * C_WIDTH
    full = lambda a: pl.BlockSpec(a.shape, lambda b, t: (0,) * a.ndim)
    return pl.pallas_call(
        functools.partial(_gdn_kernel, t_valid=t_valid, n_chunks=n_chunks), grid=(nb, nt),
        out_shape=[jax.ShapeDtypeStruct((n, C_WIDTH), BF16),
                   jax.ShapeDtypeStruct((nb, C_HEADS // 2, PAIR, PAIR), F32),
                   jax.ShapeDtypeStruct((nb, CONV_W - 1, qkv_w), F32)],
        in_specs=[pl.BlockSpec((tc, 4 * C_WIDTH), lambda b, t: (b * nt + t, 0)),
                  pl.BlockSpec((n_chunks, N_SMALL, GDN_CHUNK), lambda b, t: (b * nt + t, 0, 0)),
                  pl.BlockSpec((None, CONV_W - 1, qkv_w), lambda b, t: (b, 0, 0)),
                  pl.BlockSpec((None, C_HEADS // 2, PAIR, PAIR), lambda b, t: (b, 0, 0, 0)),
                  full(cw), full(alog), full(dt), full(ng), full(ones_hd)],
        out_specs=[pl.BlockSpec((tc, C_WIDTH), lambda b, t: (b * nt + t, 0)),
                   pl.BlockSpec((None, C_HEADS // 2, PAIR, PAIR), lambda b, t: (b, 0, 0, 0)),
                   pl.BlockSpec((None, CONV_W - 1, qkv_w), lambda b, t: (b, 0, 0))],
        scratch_shapes=[pltpu.VMEM((CONV_PAD + tc, qkv_w), F32),
                        pltpu.VMEM((C_HEADS // 2, PAIR, PAIR), F32),
                        pltpu.VMEM((n_chunks * GDN_CHUNK, C_WIDTH), F32),
                        pltpu.VMEM((n_chunks * GDN_CHUNK, qkv_w), F32)],
        compiler_params=_params(("parallel", "arbitrary")), name="gdn",
    )(zc, scal, buf0, s0, cw, alog, dt, ng, ones_hd)


def _outproj_kernel(h_ref, ya_ref, yb_ref, yc_ref, wo_ref, g_ref, wq_ref, hn_ref, x2_ref, q_ref):
    mix = jnp.concatenate([ya_ref[...], yb_ref[...], yc_ref[...]], axis=1)
    h = h_ref[...] + _dot(mix, wo_ref[...])
    hn_ref[...] = h
    x2 = _rms(h, g_ref[...]).astype(BF16)
    x2_ref[...] = x2
    q_ref[...] = _dot(x2, wq_ref[...])


def out_proj(h, ya, yb, yc, wo, g, wq, tm):
    n = h.shape[0]
    row = lambda w: pl.BlockSpec((tm, w), lambda i: (i, 0))
    full = lambda a: pl.BlockSpec(a.shape, lambda i: (0,) * a.ndim)
    nq = wq.shape[1]
    return pl.pallas_call(
        _outproj_kernel, grid=(n // tm,),
        out_shape=[jax.ShapeDtypeStruct((n, D_MODEL), F32), jax.ShapeDtypeStruct((n, D_MODEL), BF16),
                   jax.ShapeDtypeStruct((n, nq), F32)],
        in_specs=[row(D_MODEL), row(A_WIDTH), row(B_WIDTH), row(C_WIDTH), full(wo), full(g), full(wq)],
        out_specs=[row(D_MODEL), row(D_MODEL), row(nq)],
        compiler_params=_params(("parallel",)), name="out_proj",
    )(h, ya, yb, yc, wo, g, wq)


def _top16_rows(s, payload):
    nrow = s.shape[0]
    row = _iota(s.shape, 0)
    vals, pays = [], []
    for _ in range(P_TOPK):
        m = jnp.max(s, axis=0, keepdims=True)
        pos = jnp.min(jnp.where(s == m, row, nrow), axis=0, keepdims=True)
        hit = row == pos
        vals.append(m)
        pays.append(jnp.max(jnp.where(hit, payload, -1), axis=0, keepdims=True))
        s = jnp.where(hit, -jnp.inf, s)
    return jnp.concatenate(vals, axis=0), jnp.concatenate(pays, axis=0)


def _peer_topk_kernel(q_ref, keys_ref, eidx_ref, gate_ref):
    tn = q_ref.shape[0]
    key_row = _iota((N_KEYS, tn), 0)
    eidx, gates = [], []
    for h in range(P_HEADS):
        tops = []
        for half in range(2):
            col = (2 * h + half) * P_DK
            s = _dot_nt(keys_ref[h, half], q_ref[:, col:col + P_DK], HI)
            tops.append(_top16_rows(s, key_row))
        (s0, i0), (s1, i1) = tops
        cand = (s0[:, None, :] + s1[None, :, :]).reshape(P_TOPK * P_TOPK, tn)
        cidx = (i0[:, None, :] * N_KEYS + i1[None, :, :]).reshape(P_TOPK * P_TOPK, tn)
        best, e = _top16_rows(cand, cidx)
        w = jnp.exp(best - best[0:1, :])
        gates.append(w / jnp.sum(w, axis=0, keepdims=True))
        eidx.append(e)
    eidx_ref[...] = jnp.concatenate(eidx, axis=0).T
    gate_ref[...] = jnp.concatenate(gates, axis=0).T


def peer_topk(q, keys, tn):
    n = q.shape[0]
    hk = P_HEADS * P_TOPK
    return pl.pallas_call(
        _peer_topk_kernel, grid=(n // tn,),
        out_shape=[jax.ShapeDtypeStruct((n, hk), I32), jax.ShapeDtypeStruct((n, hk), F32)],
        in_specs=[pl.BlockSpec((tn, q.shape[1]), lambda i: (i, 0)), pl.BlockSpec(keys.shape, lambda i: (0, 0, 0, 0))],
        out_specs=[pl.BlockSpec((tn, hk), lambda i: (i, 0))] * 2,
        compiler_params=_params(("parallel",)), name="peer_topk",
    )(q, keys)


E_CHUNK = 1024
SLABS = E_CHUNK // N_KEYS
KEY_BITS = N_KEYS.bit_length() - 1


def _peer_expert_kernel(x_ref, h_ref, eidx_ref, gate_ref, u_ref, v_ref, o_ref, pf_ref, hm_ref, acc_ref):
    ph = pl.program_id(1)
    ci = pl.program_id(2)
    nch = pl.num_programs(2)
    tn = x_ref.shape[0]
    hk = P_HEADS * P_TOPK

    @pl.when((ph == 0) & (ci == 0))
    def _():
        pf_ref[...] = jnp.zeros(pf_ref.shape, F32)

    @pl.when(ph == 0)
    def _():
        e = eidx_ref[...]
        ii = jnp.right_shift(e, KEY_BITS)
        jj = jnp.bitwise_and(e, N_KEYS - 1)
        pc = _dot_nt(x_ref[...], u_ref[...])
        pf = pf_ref[...]
        for s in range(SLABS):
            got = jnp.take_along_axis(pc[:, s * N_KEYS:(s + 1) * N_KEYS], jj, axis=1)
            pf = jnp.where(ii == ci * SLABS + s, got, pf)
        pf_ref[...] = pf

    @pl.when((ph == 0) & (ci == nch - 1))
    def _():
        hid = _gelu_erf(pf_ref[...]) * gate_ref[...]
        pf_ref[...] = hid
        sub = _iota((N_KEYS, hk), 0)

        def token(n, carry):
            e = eidx_ref[pl.ds(n, 1), :]
            hrow = pf_ref[pl.ds(n, 1), :]
            w = jnp.where(sub == jnp.right_shift(e, KEY_BITS), hrow, 0.0).astype(BF16)
            bt = jnp.where(sub == jnp.bitwise_and(e, N_KEYS - 1), 1.0, 0.0).astype(BF16)
            hm_ref[pl.ds(pl.multiple_of(n * N_KEYS, N_KEYS), N_KEYS), :] = _dot_nt(w, bt)
            return carry

        lax.fori_loop(0, tn, token, 0)
        acc_ref[...] = jnp.zeros(acc_ref.shape, F32)

    @pl.when(ph == 1)
    def _():
        parts = [hm_ref[pl.ds(ci * SLABS + s, tn, stride=N_KEYS), :] for s in range(SLABS)]
        hmat = jnp.concatenate(parts, axis=1).astype(BF16)
        acc_ref[...] += _dot(hmat, v_ref[...])

    @pl.when((ph == 1) & (ci == nch - 1))
    def _():
        o_ref[...] = h_ref[...] + acc_ref[...]


def peer_experts(x2, h, eidx, gate, u, v, tn):
    n = x2.shape[0]
    hk = P_HEADS * P_TOPK
    nch = N_EXPERTS // E_CHUNK
    row = lambda w: pl.BlockSpec((tn, w), lambda i, ph, c: (i, 0))
    return pl.pallas_call(
        _peer_expert_kernel, grid=(n // tn, 2, nch),
        out_shape=jax.ShapeDtypeStruct((n, D_MODEL), F32),
        in_specs=[row(D_MODEL), row(D_MODEL), row(hk), row(hk),
                  pl.BlockSpec((E_CHUNK, D_MODEL), lambda i, ph, c: (jnp.where(ph == 0, c, nch - 1), 0)),
                  pl.BlockSpec((E_CHUNK, D_MODEL), lambda i, ph, c: (jnp.where(ph == 0, 0, c), 0))],
        out_specs=row(D_MODEL),
        scratch_shapes=[pltpu.VMEM((tn, hk), F32), pltpu.VMEM((tn * N_KEYS, N_KEYS), F32),
                        pltpu.VMEM((tn, D_MODEL), F32)],
        compiler_params=_params(("parallel", "arbitrary", "arbitrary")), name="peer_experts",
    )(x2, h, eidx, gate, u, v)


def _final_norm_kernel(h_ref, g_ref, o_ref):
    o_ref[...] = _rms(h_ref[...], g_ref[...])


def final_norm(h, g, tm):
    n = h.shape[0]
    row = pl.BlockSpec((tm, D_MODEL), lambda i: (i, 0))
    return pl.pallas_call(
        _final_norm_kernel, grid=(n // tm,), out_shape=jax.ShapeDtypeStruct(h.shape, F32),
        in_specs=[row, pl.BlockSpec(g.shape, lambda i: (0, 0))], out_specs=row,
        compiler_params=_params(("parallel",)), name="final_norm",
    )(h, g)


def _block_diag(w):
    g, a, b = w.shape
    eye = jnp.eye(g, dtype=w.dtype)
    return (eye[:, None, :, None] * w[:, :, None, :]).reshape(g * a, g * b)


def _prep_layer(W, l):
    w_in = W["w_in"][l]
    return dict(
        g1=W["norm1_g"][l][None, :],
        wa=w_in[:, OFF_AQ:OFF_AF].astype(BF16),
        wb=w_in[:, OFF_BX:OFF_CQKV].astype(BF16),
        wc=jnp.concatenate([w_in[:, OFF_CQKV:OFF_CA], w_in[:, OFF_CG:D_IN]], axis=1).astype(BF16),
        ws=jnp.concatenate([w_in[:, OFF_AF:OFF_BX], w_in[:, OFF_CA:OFF_CG]], axis=1).T,
        bf=W["b_f"][l][:, None],
        lru_cw=W["lru_conv_w"][l], lru_cb=W["lru_conv_b"][l][None, :],
        lru_wa=_block_diag(W["lru_wa"][l]), lru_ba=W["lru_ba"][l][None, :],
        lru_wx=_block_diag(W["lru_wx"][l]), lru_bx=W["lru_bx"][l][None, :],
        lru_lam=W["lru_lambda"][l][None, :],
        gdn_cw=W["gdn_conv_w"][l], gdn_alog=W["gdn_A_log"][l][:, None], gdn_dt=W["gdn_dt_bias"][l][:, None],
        gdn_ng=jnp.tile(W["gdn_norm_g"][l], C_HEADS)[None, :],
        wo=W["w_out"][l].astype(BF16), g2=W["norm2_g"][l][None, :], wq=W["peer_wq"][l].astype(BF16),
        keys=W["peer_keys"][l], u=W["peer_u"][l].astype(BF16), v=W["peer_v"][l].astype(BF16),
    )


def _pair_state(s):
    nb = s.shape[0]
    s = s.reshape(nb, C_HEADS // 2, 2, C_HEAD_DIM, C_HEAD_DIM)
    eye = jnp.eye(2, dtype=s.dtype)
    return (s[:, :, :, :, None, :] * eye[None, None, :, None, :, None]).reshape(nb, C_HEADS // 2, PAIR, PAIR)


def _unpair_state(s):
    nb = s.shape[0]
    s = s.reshape(nb, C_HEADS // 2, 2, C_HEAD_DIM, 2, C_HEAD_DIM)
    return jnp.stack([s[:, :, 0, :, 0, :], s[:, :, 1, :, 1, :]], axis=2).reshape(nb, C_HEADS, C_HEAD_DIM, C_HEAD_DIM)


def _trunk(x, nb, t_real, paged, lru_h, lru_buf, gdn_s, gdn_buf, layers, final_g, cfg):
    t_rows = x.shape[1]
    n = nb * t_rows
    h = x.reshape(n, D_MODEL)
    ones_hd = _block_diag(jnp.full((C_HEADS, C_HEAD_DIM, C_HEAD_DIM), 1.0 / C_HEAD_DIM, F32))
    outs = []
    for l, lw in enumerate(layers):
        q, k, v, kb, vb, zb, zc, st = in_proj(h, lw["g1"], lw["wa"], lw["wb"], lw["wc"], lw["ws"], cfg["tm"])
        if paged is None:
            lf, c = logf_rows(st, lw["bf"], nb, True)
            ya = attn_prompt(q, kb, vb, c.reshape(A_HEADS // 2, 2, n), nb, cfg["tq"])
            n_chunks = cfg["tc_gdn"] // GDN_CHUNK
            scal = st.reshape(N_SMALL, n // GDN_CHUNK, GDN_CHUNK).transpose(1, 0, 2)
        else:
            ck, cv, clt, pt = paged
            lf, _ = logf_rows(st, lw["bf"], 1, False)
            lfn = lf.reshape(A_HEADS, nb, t_rows).transpose(1, 0, 2)
            lfn = jnp.where(jnp.arange(t_rows) < t_real, lfn, 0.0)
            lfn = jnp.pad(lfn, ((0, 0), (0, 0), (0, PAGE_SIZE - t_rows)))
            ya = attn_sample(pt, q, ck, cv, clt, k, v, lfn, l, t_real)
            n_chunks = 1
            scal = jnp.pad(st.reshape(N_SMALL, nb, t_rows).transpose(1, 0, 2),
                           ((0, 0), (0, 0), (0, GDN_CHUNK - t_rows)))
        t_valid = min(t_real, cfg["tc_lru"])
        yb, lru_ht, lru_nb = lru_mixer(zb, lru_buf[l], lru_h[l][:, None, :], lw["lru_cw"], lw["lru_cb"],
                                       lw["lru_wa"], lw["lru_ba"], lw["lru_wx"], lw["lru_bx"], lw["lru_lam"],
                                       nb, cfg["tc_lru"], t_valid)
        t_valid = min(t_real, cfg["tc_gdn"])
        yc, gdn_st, gdn_nb = gdn_mixer(zc, scal, gdn_buf[l], _pair_state(gdn_s[l]), lw["gdn_cw"], lw["gdn_alog"],
                                       lw["gdn_dt"], lw["gdn_ng"], ones_hd, nb, cfg["tc_gdn"], t_valid, n_chunks)
        hmid, x2, qp = out_proj(h, ya, yb, yc, lw["wo"], lw["g2"], lw["wq"], cfg["tm"])
        eidx, gate = peer_topk(qp, lw["keys"], cfg["tn_topk"])
        h = peer_experts(x2, hmid, eidx, gate, lw["u"], lw["v"], cfg["tn_peer"])
        outs.append((k.reshape(nb, t_rows, A_HEADS, A_HEAD_DIM)[:, :t_real],
                     v.reshape(nb, t_rows, A_HEADS, A_HEAD_DIM)[:, :t_real],
                     lf.reshape(A_HEADS, nb, t_rows).transpose(1, 2, 0)[:, :t_real],
                     lru_ht[:, 0, :], lru_nb, _unpair_state(gdn_st), gdn_nb))
    y = final_norm(h, final_g[None, :], cfg["tm"]).reshape(nb, t_rows, D_MODEL)[:, :t_real]
    return y, tuple(jnp.stack([o[i] for o in outs], axis=0) for i in range(len(outs[0])))


PROMPT_CFG = dict(tm=512, tq=512, tc_lru=512, tc_gdn=256, tn_topk=256, tn_peer=256)
SAMPLE_CFG = dict(tm=256, tq=None, tc_lru=SAMPLE_ROWS, tc_gdn=SAMPLE_ROWS, tn_topk=256, tn_peer=256)


def kernel(x_prompt, x_sample, cache_k, cache_v, cache_logf, page_table, state_lru_h, state_lru_conv,
           state_gdn_S, state_gdn_conv, norm1_g, w_in, b_f, lru_conv_w, lru_conv_b, lru_wa, lru_ba,
           lru_wx, lru_bx, lru_lambda, gdn_conv_w, gdn_A_log, gdn_dt_bias, gdn_norm_g, w_out, norm2_g,
           peer_wq, peer_keys, peer_u, peer_v, final_norm_g):
    W = dict(norm1_g=norm1_g, w_in=w_in, b_f=b_f, lru_conv_w=lru_conv_w, lru_conv_b=lru_conv_b, lru_wa=lru_wa,
             lru_ba=lru_ba, lru_wx=lru_wx, lru_bx=lru_bx, lru_lambda=lru_lambda, gdn_conv_w=gdn_conv_w,
             gdn_A_log=gdn_A_log, gdn_dt_bias=gdn_dt_bias, gdn_norm_g=gdn_norm_g, w_out=w_out, norm2_g=norm2_g,
             peer_wq=peer_wq, peer_keys=peer_keys, peer_u=peer_u, peer_v=peer_v)
    depth = w_in.shape[0]
    layers = [_prep_layer(W, l) for l in range(depth)]
    bp, tp = x_prompt.shape[:2]
    bs, ts = x_sample.shape[:2]

    zeros = lambda *s: jnp.zeros((depth, bp) + s, F32)
    yp, sp = _trunk(x_prompt, bp, tp, None, zeros(B_WIDTH), zeros(CONV_W - 1, B_WIDTH),
                    zeros(C_HEADS, C_HEAD_DIM, C_HEAD_DIM), zeros(CONV_W - 1, 3 * C_WIDTH),
                    layers, final_norm_g, PROMPT_CFG)

    n_pool = cache_k.shape[1]
    ck = cache_k.reshape(depth, n_pool, PAGE_SIZE, A_WIDTH)
    cv = cache_v.reshape(depth, n_pool, PAGE_SIZE, A_WIDTH)
    clt = cache_logf.transpose(0, 1, 3, 2)
    xs = jnp.pad(x_sample, ((0, 0), (0, SAMPLE_ROWS - ts), (0, 0)))
    ys, ss = _trunk(xs, bs, ts, (ck, cv, clt, page_table), state_lru_h, state_lru_conv, state_gdn_S,
                    state_gdn_conv, layers, final_norm_g, SAMPLE_CFG)

    return (yp, ys, sp[0], sp[1], sp[2], ss[0], ss[1], ss[2], sp[3], ss[3], sp[4], ss[4], sp[5], ss[5], sp[6], ss[6])
```

```python
import functools

import jax
import jax.numpy as jnp
from jax import lax
from jax.experimental import pallas as pl
from jax.experimental.pallas import tpu as pltpu

F32 = jnp.float32
BF16 = jnp.bfloat16
I32 = jnp.int32
HI = lax.Precision.HIGHEST

D_MODEL = 1024
A_HEADS = 8
A_HEAD_DIM = 64
A_WIDTH = A_HEADS * A_HEAD_DIM
B_WIDTH = 256
B_BLOCKS = 4
LRU_C = 8.0
C_HEADS = 4
C_HEAD_DIM = 64
C_WIDTH = C_HEADS * C_HEAD_DIM
GDN_CHUNK = 64
CONV_W = 4
P_HEADS = 8
P_DK = 128
N_KEYS = 128
N_EXPERTS = N_KEYS * N_KEYS
P_TOPK = 16
PAGE_SIZE = 128
EPS = 1e-6
SAMPLE_ROWS = 8
N_SMALL = 16

OFF_AQ = 0
OFF_AF = OFF_AQ + 3 * A_WIDTH
OFF_BX = OFF_AF + A_HEADS
OFF_CQKV = OFF_BX + 2 * B_WIDTH
OFF_CA = OFF_CQKV + 3 * C_WIDTH
OFF_CG = OFF_CA + 2 * C_HEADS
D_IN = OFF_CG + C_WIDTH

VMEM_LIMIT = 48 * 1024 * 1024


def _params(sem, vmem=VMEM_LIMIT):
    return pltpu.CompilerParams(dimension_semantics=sem, vmem_limit_bytes=vmem)


def _dot(a, b, precision=None):
    return jnp.dot(a, b, precision=precision, preferred_element_type=F32)


def _dot_nt(a, b, precision=None):
    return lax.dot_general(a, b, (((1,), (1,)), ((), ())), precision=precision, preferred_element_type=F32)


def _dot_tn(a, b, precision=None):
    return lax.dot_general(a, b, (((0,), (0,)), ((), ())), precision=precision, preferred_element_type=F32)


_DOTS = {"nn": _dot, "nt": _dot_nt, "tn": _dot_tn}


def _split_bf16(a):
    hi = a.astype(BF16)
    return hi, (a - hi.astype(F32)).astype(BF16)


def _mm(a, b, prec, kind="nn"):
    dot = _DOTS[kind]
    if prec == "bf":
        return dot(a.astype(BF16), b.astype(BF16))
    if prec == "x2l":
        ah, al = _split_bf16(a)
        bb = b.astype(BF16)
        return dot(ah, bb) + dot(al, bb)
    if prec == "x3":
        ah, al = _split_bf16(a)
        bh, bl = _split_bf16(b)
        return dot(ah, bh) + (dot(ah, bl) + dot(al, bh))
    return dot(a, b, HI)


def _rms(x, g):
    return x * lax.rsqrt(jnp.mean(x * x, axis=-1, keepdims=True) + EPS) * g


def _gelu_erf(x):
    return 0.5 * x * (1.0 + lax.erf(x * 0.7071067811865476))


def _iota(shape, dim):
    return lax.broadcasted_iota(I32, shape, dim)


def _inproj_kernel(x_ref, g_ref, wa_ref, wb_ref, wc_ref, ws_ref,
                   q_ref, k_ref, v_ref, kb_ref, vb_ref, zb_ref, zc_ref, st_ref):
    hn = _rms(x_ref[...], g_ref[...])
    hb = hn.astype(BF16)
    za = _dot(hb, wa_ref[...])
    q_ref[...] = (za[:, :A_WIDTH] * (A_HEAD_DIM ** -0.5)).astype(BF16)
    k = za[:, A_WIDTH:2 * A_WIDTH]
    v = za[:, 2 * A_WIDTH:]
    k_ref[...] = k
    v_ref[...] = v
    kb_ref[...] = k.astype(BF16)
    vb_ref[...] = v.astype(BF16)
    zb_ref[...] = _dot(hb, wb_ref[...])
    zc_ref[...] = _dot(hb, wc_ref[...])
    st_ref[...] = _dot_nt(ws_ref[...], hn, HI)


def in_proj(h, g, wa, wb, wc, ws, tm):
    n = h.shape[0]
    grid = (n // tm,)
    row = lambda w: pl.BlockSpec((tm, w), lambda i: (i, 0))
    full = lambda a: pl.BlockSpec(a.shape, lambda i: (0,) * a.ndim)
    out_shape = [
        jax.ShapeDtypeStruct((n, A_WIDTH), BF16),
        jax.ShapeDtypeStruct((n, A_WIDTH), F32),
        jax.ShapeDtypeStruct((n, A_WIDTH), F32),
        jax.ShapeDtypeStruct((n, A_WIDTH), BF16),
        jax.ShapeDtypeStruct((n, A_WIDTH), BF16),
        jax.ShapeDtypeStruct((n, 2 * B_WIDTH), F32),
        jax.ShapeDtypeStruct((n, 4 * C_WIDTH), F32),
        jax.ShapeDtypeStruct((N_SMALL, n), F32),
    ]
    out_specs = [row(A_WIDTH)] * 5 + [row(2 * B_WIDTH), row(4 * C_WIDTH),
                                      pl.BlockSpec((N_SMALL, tm), lambda i: (0, i))]
    return pl.pallas_call(
        _inproj_kernel, grid=grid, out_shape=out_shape,
        in_specs=[row(D_MODEL), full(g), full(wa), full(wb), full(wc), full(ws)],
        out_specs=out_specs, compiler_params=_params(("parallel",)), name="in_proj",
    )(h, g, wa, wb, wc, ws)


CUMSUM_CHUNK = 512


def _logf_kernel(st_ref, bf_ref, lf_ref, c_ref, *, cumsum):
    lf = jax.nn.log_sigmoid(st_ref[0:A_HEADS, :] + bf_ref[...])
    lf_ref[...] = lf
    if not cumsum:
        c_ref[...] = lf
        return
    t = lf.shape[1]
    ch = min(CUMSUM_CHUNK, t)
    tri = (_iota((ch, ch), 0) <= _iota((ch, ch), 1)).astype(F32)
    carry = jnp.zeros((A_HEADS, 1), F32)
    for j in range(t // ch):
        cs = _dot(lf[:, j * ch:(j + 1) * ch], tri, HI) + carry
        c_ref[:, j * ch:(j + 1) * ch] = cs
        carry = cs[:, ch - 1:ch]


def logf_rows(st, bf, nb, cumsum):
    n = st.shape[1]
    t = n // nb
    spec_in = pl.BlockSpec((N_SMALL, t), lambda b: (0, b))
    spec_out = pl.BlockSpec((A_HEADS, t), lambda b: (0, b))
    return pl.pallas_call(
        functools.partial(_logf_kernel, cumsum=cumsum), grid=(nb,),
        out_shape=[jax.ShapeDtypeStruct((A_HEADS, n), F32)] * 2,
        in_specs=[spec_in, pl.BlockSpec(bf.shape, lambda b: (0, 0))],
        out_specs=[spec_out, spec_out], compiler_params=_params(("parallel",)), name="logf",
    )(st, bf)


def _attn_prompt_kernel(qi_ref, ki_ref, q_ref, k_ref, v_ref, c_ref, o_ref, m_ref, l_ref, acc_ref):
    step = pl.program_id(2)
    qi = qi_ref[step]
    ki = ki_ref[step]
    tq, tk = q_ref.shape[0], k_ref.shape[0]
    w = 2 * A_HEAD_DIM
    lane = _iota((1, w), 1)

    @pl.when(ki == 0)
    def _():
        m_ref[...] = jnp.full(m_ref.shape, -jnp.inf, F32)
        l_ref[...] = jnp.zeros(l_ref.shape, F32)
        acc_ref[...] = jnp.zeros(acc_ref.shape, F32)

    def block(masked):
        q = q_ref[...]
        k = k_ref[...]
        v = v_ref[...]
        for h in range(2):
            head = (lane >= A_HEAD_DIM) if h else (lane < A_HEAD_DIM)
            s = _dot_nt(jnp.where(head, q, jnp.zeros_like(q)), k) - c_ref[0, h:h + 1, :]
            if masked:
                s = jnp.where(_iota((tq, tk), 1) <= _iota((tq, tk), 0), s, -jnp.inf)
            m_prev = m_ref[h]
            m_new = jnp.maximum(m_prev, jnp.max(s, axis=1, keepdims=True))
            alpha = jnp.exp(m_prev - m_new)
            p = jnp.exp(s - jnp.concatenate([m_new] * (tk // w), axis=1))
            l_ref[h] = alpha * l_ref[h] + jnp.sum(p, axis=1, keepdims=True)
            acc_ref[h] = alpha * acc_ref[h] + _dot(p.astype(BF16), v)
            m_ref[h] = m_new

    @pl.when(ki < qi)
    def _():
        block(False)

    @pl.when(ki == qi)
    def _():
        block(True)
        o0 = acc_ref[0] / l_ref[0]
        o1 = acc_ref[1] / l_ref[1]
        o_ref[...] = jnp.where(lane < A_HEAD_DIM, o0, o1).astype(o_ref.dtype)


def attn_prompt(q, kb, vb, c, nb, tq):
    n = q.shape[0]
    t = n // nb
    nq = t // tq
    w = 2 * A_HEAD_DIM
    pairs = [(i, j) for i in range(nq) for j in range(i + 1)]
    qi_list = jnp.asarray([i for i, _ in pairs], I32)
    ki_list = jnp.asarray([j for _, j in pairs], I32)
    qspec = pl.BlockSpec((tq, w), lambda b, p, s, qi, ki: (b * nq + qi[s], p))
    kspec = pl.BlockSpec((tq, w), lambda b, p, s, qi, ki: (b * nq + ki[s], p))
    cspec = pl.BlockSpec((1, 2, tq), lambda b, p, s, qi, ki: (p, 0, b * nq + ki[s]))
    grid_spec = pltpu.PrefetchScalarGridSpec(
        num_scalar_prefetch=2, grid=(nb, A_HEADS // 2, len(pairs)),
        in_specs=[qspec, kspec, kspec, cspec], out_specs=qspec,
        scratch_shapes=[pltpu.VMEM((2, tq, w), F32), pltpu.VMEM((2, tq, w), F32), pltpu.VMEM((2, tq, w), F32)])
    return pl.pallas_call(
        _attn_prompt_kernel, grid_spec=grid_spec,
        out_shape=jax.ShapeDtypeStruct((n, A_WIDTH), BF16),
        compiler_params=_params(("parallel", "parallel", "arbitrary")), name="attn_prompt",
    )(qi_list, ki_list, q, kb, vb, c)


def _attn_sample_kernel(pt_ref, q_ref, ck_ref, cv_ref, cl_ref, kn_ref, vn_ref, ln_ref, o_ref,
                        qh_ref, m_ref, l_ref, acc_ref, carry_ref, *, n_pages, t_new):
    del pt_ref
    j = pl.program_id(1)
    nr = A_HEADS * SAMPLE_ROWS
    hd = A_HEAD_DIM

    @pl.when(j == 0)
    def _():
        q = q_ref[...].astype(F32)
        for h in range(A_HEADS):
            qh_ref[h] = q[:, h * hd:(h + 1) * hd]
        m_ref[...] = jnp.full(m_ref.shape, -jnp.inf, F32)
        l_ref[...] = jnp.zeros(l_ref.shape, F32)
        acc_ref[...] = jnp.zeros(acc_ref.shape, F32)
        carry_ref[...] = jnp.zeros(carry_ref.shape, F32)

    tri = (_iota((PAGE_SIZE, PAGE_SIZE), 0) <= _iota((PAGE_SIZE, PAGE_SIZE), 1)).astype(F32)

    def page(k_head, v_head, lf, visible):
        c = _dot(lf, tri, HI) + carry_ref[...]
        carry_ref[...] = c[:, PAGE_SIZE - 1:PAGE_SIZE]
        s = jnp.concatenate([_dot_nt(qh_ref[h].astype(BF16), k_head(h).astype(BF16)) - c[h:h + 1, :]
                             for h in range(A_HEADS)], axis=0)
        if visible is not None:
            s = jnp.where(visible, s, -jnp.inf)
        m_prev = m_ref[...]
        m_new = jnp.maximum(m_prev, jnp.max(s, axis=1, keepdims=True))
        alpha = jnp.exp(m_prev - m_new)
        p = jnp.exp(s - m_new)
        l_ref[...] = alpha * l_ref[...] + jnp.sum(p, axis=1, keepdims=True)
        pv = jnp.concatenate([_dot(p[h * SAMPLE_ROWS:(h + 1) * SAMPLE_ROWS, :].astype(BF16), v_head(h).astype(BF16))
                              for h in range(A_HEADS)], axis=0)
        acc_ref[...] = alpha * acc_ref[...] + pv
        m_ref[...] = m_new

    @pl.when(j < n_pages)
    def _():
        page(lambda h: ck_ref[pl.ds(h, PAGE_SIZE, stride=A_HEADS), :],
             lambda h: cv_ref[pl.ds(h, PAGE_SIZE, stride=A_HEADS), :], cl_ref[...], None)

    @pl.when(j == n_pages)
    def _():
        pad = jnp.zeros((PAGE_SIZE - SAMPLE_ROWS, hd), F32)
        new_head = lambda ref: (lambda h: jnp.concatenate([ref[:, h * hd:(h + 1) * hd], pad], axis=0))
        visible = _iota((nr, PAGE_SIZE), 1) <= _iota((nr, PAGE_SIZE), 0) % SAMPLE_ROWS
        page(new_head(kn_ref), new_head(vn_ref), ln_ref[...], visible)
        o = acc_ref[...] / l_ref[...]
        o = jnp.concatenate([o[h * SAMPLE_ROWS:(h + 1) * SAMPLE_ROWS, :] for h in range(A_HEADS)], axis=1)
        o_ref[...] = jnp.where(_iota(o.shape, 0) < t_new, o, 0.0).astype(o_ref.dtype)


def attn_sample(page_table, q, cache_k, cache_v, cache_lt, kn, vn, lfn, layer, t_new):
    nb, n_pages = page_table.shape
    nr = A_HEADS * SAMPLE_ROWS
    last = n_pages - 1
    seq = lambda w: pl.BlockSpec((SAMPLE_ROWS, w), lambda b, j, pt: (b, 0))
    pg = lambda r, w: pl.BlockSpec((None, None, r, w),
                                   lambda b, j, pt: (layer, pt[b, jnp.minimum(j, last)], 0, 0))
    grid_spec = pltpu.PrefetchScalarGridSpec(
        num_scalar_prefetch=1, grid=(nb, n_pages + 1),
        in_specs=[seq(A_WIDTH), pg(PAGE_SIZE * A_HEADS, A_HEAD_DIM), pg(PAGE_SIZE * A_HEADS, A_HEAD_DIM),
                  pg(A_HEADS, PAGE_SIZE),
                  seq(A_WIDTH), seq(A_WIDTH), pl.BlockSpec((None, A_HEADS, PAGE_SIZE), lambda b, j, pt: (b, 0, 0))],
        out_specs=seq(A_WIDTH),
        scratch_shapes=[pltpu.VMEM((A_HEADS, SAMPLE_ROWS, A_HEAD_DIM), F32), pltpu.VMEM((nr, 1), F32),
                        pltpu.VMEM((nr, 1), F32), pltpu.VMEM((nr, A_HEAD_DIM), F32), pltpu.VMEM((A_HEADS, 1), F32)])
    return pl.pallas_call(
        functools.partial(_attn_sample_kernel, n_pages=n_pages, t_new=t_new), grid_spec=grid_spec,
        out_shape=jax.ShapeDtypeStruct(q.shape, BF16),
        compiler_params=_params(("parallel", "arbitrary")), name="attn_sample",
    )(page_table, q, cache_k, cache_v, cache_lt, kn, vn, lfn)


CONV_PAD = 8


def _conv_block(xp_ref, x, w_ref, tc):
    xp_ref[CONV_PAD:CONV_PAD + tc, :] = x
    y = xp_ref[CONV_PAD - 3:CONV_PAD - 3 + tc, :] * w_ref[0:1, :]
    for i in range(1, CONV_W):
        y = y + xp_ref[CONV_PAD - 3 + i:CONV_PAD - 3 + i + tc, :] * w_ref[i:i + 1, :]
    return y


def _lru_kernel(z_ref, buf_ref, h0_ref, cw_ref, cb_ref, wa_ref, ba_ref, wx_ref, bx_ref, lam_ref,
                y_ref, ht_ref, nb_ref, xp_ref, hc_ref, *, t_valid):
    ti = pl.program_id(1)
    tc = z_ref.shape[0]

    @pl.when(ti == 0)
    def _():
        xp_ref[CONV_PAD - 3:CONV_PAD, :] = buf_ref[...]
        hc_ref[...] = h0_ref[...]

    x = z_ref[:, :B_WIDTH]
    xc = _conv_block(xp_ref, x, cw_ref, tc) + cb_ref[...]
    r = jax.nn.sigmoid(_dot(xc, wa_ref[...], HI) + ba_ref[...])
    ig = jax.nn.sigmoid(_dot(xc, wx_ref[...], HI) + bx_ref[...])
    log_a = -LRU_C * r * jax.nn.softplus(-lam_ref[...])
    a = jnp.exp(log_a)
    b = jnp.sqrt(1.0 - jnp.exp(2.0 * log_a)) * (ig * xc)
    row = _iota((tc, B_WIDTH), 0)
    d = 1
    while d < tc:
        keep = row >= d
        a_prev = jnp.where(keep, pltpu.roll(a, d, 0), 1.0)
        b_prev = jnp.where(keep, pltpu.roll(b, d, 0), 0.0)
        b = a * b_prev + b
        a = a * a_prev
        d *= 2
    hs = a * hc_ref[...] + b
    y_ref[...] = (hs * jax.nn.gelu(z_ref[:, B_WIDTH:])).astype(y_ref.dtype)
    hc_ref[...] = hs[t_valid - 1:t_valid, :]
    xp_ref[CONV_PAD - 3:CONV_PAD, :] = xp_ref[CONV_PAD + t_valid - 3:CONV_PAD + t_valid, :]

    @pl.when(ti == pl.num_programs(1) - 1)
    def _():
        ht_ref[...] = hc_ref[...]
        nb_ref[...] = xp_ref[CONV_PAD - 3:CONV_PAD, :]


def lru_mixer(zb, buf0, h0, cw, cb, wa, ba, wx, bx, lam, nb, tc, t_valid):
    n = zb.shape[0]
    nt = n // nb // tc
    full = lambda a: pl.BlockSpec(a.shape, lambda b, t: (0,) * a.ndim)
    per_b = lambda r: pl.BlockSpec((None, r, B_WIDTH), lambda b, t: (b, 0, 0))
    return pl.pallas_call(
        functools.partial(_lru_kernel, t_valid=t_valid), grid=(nb, nt),
        out_shape=[jax.ShapeDtypeStruct((n, B_WIDTH), BF16), jax.ShapeDtypeStruct((nb, 1, B_WIDTH), F32),
                   jax.ShapeDtypeStruct((nb, CONV_W - 1, B_WIDTH), F32)],
        in_specs=[pl.BlockSpec((tc, 2 * B_WIDTH), lambda b, t: (b * nt + t, 0)), per_b(CONV_W - 1), per_b(1),
                  full(cw), full(cb), full(wa), full(ba), full(wx), full(bx), full(lam)],
        out_specs=[pl.BlockSpec((tc, B_WIDTH), lambda b, t: (b * nt + t, 0)), per_b(1), per_b(CONV_W - 1)],
        scratch_shapes=[pltpu.VMEM((CONV_PAD + tc, B_WIDTH), F32), pltpu.VMEM((1, B_WIDTH), F32)],
        compiler_params=_params(("parallel", "arbitrary")), name="lru",
    )(zb, buf0, h0, cw, cb, wa, ba, wx, bx, lam)


PAIR = 2 * C_HEAD_DIM
GDN_INV_PREC = "x3"
GDN_PREC = "bf"


def _stack_heads(x, lo_mask):
    return jnp.concatenate([jnp.where(lo_mask, x, 0.0), jnp.where(lo_mask, 0.0, x)], axis=0)


def _row_to_col(row, eye):
    return jnp.sum(jnp.where(eye, row, 0.0), axis=1, keepdims=True)


def _gdn_kernel(z_ref, sc_ref, buf_ref, s0_ref, cw_ref, alog_ref, dt_ref, ng_ref, ones_hd_ref,
                y_ref, st_ref, nb_ref, xp_ref, s_ref, o_ref, ys_ref, *, t_valid, n_chunks):
    ti = pl.program_id(1)
    tc = z_ref.shape[0]
    c = GDN_CHUNK
    qkv_w = 3 * C_WIDTH

    @pl.when(ti == 0)
    def _():
        xp_ref[CONV_PAD - 3:CONV_PAD, :] = buf_ref[...]
        s_ref[...] = s0_ref[...]

    y = jax.nn.silu(_conv_block(xp_ref, z_ref[:, :qkv_w], cw_ref, tc))
    xp_ref[CONV_PAD - 3:CONV_PAD, :] = xp_ref[CONV_PAD + t_valid - 3:CONV_PAD + t_valid, :]
    padded = tc < n_chunks * c
    if padded:
        y = jnp.concatenate([y, jnp.zeros((n_chunks * c - tc, qkv_w), F32)], axis=0)
    ones_hd = ones_hd_ref[...]

    def l2n(x):
        return x * lax.rsqrt(_mm(x * x, ones_hd, "x2l") * C_HEAD_DIM + EPS)

    ys_ref[:, :C_WIDTH] = l2n(y[:, :C_WIDTH]) * (C_HEAD_DIM ** -0.5)
    ys_ref[:, C_WIDTH:2 * C_WIDTH] = l2n(y[:, C_WIDTH:2 * C_WIDTH])
    ys_ref[:, 2 * C_WIDTH:] = y[:, 2 * C_WIDTH:]

    r128 = _iota((PAIR, PAIR), 0)
    c128 = _iota((PAIR, PAIR), 1)
    same = (r128 // c) == (c128 // c)
    eye = r128 == c128
    incl = same & (c128 <= r128)
    strict = same & (c128 < r128)
    tri_bd = (same & (r128 <= c128)).astype(F32)
    last_sel = same & (c128 % c == c - 1)
    lo_lane = _iota((1, PAIR), 1) < c
    lane_t = _iota((1, PAIR), 1) % c
    ident = eye.astype(F32)

    def chunk(ci, carry):
        r0 = pl.multiple_of(ci * c, c)
        sc = sc_ref[ci]
        g4 = -jnp.exp(alog_ref[...]) * jax.nn.softplus(sc[8:12, :] + dt_ref[...])
        b4 = jax.nn.sigmoid(sc[12:16, :])
        for p in range(C_HEADS // 2):
            g_row = jnp.concatenate([g4[2 * p:2 * p + 1, :], g4[2 * p + 1:2 * p + 2, :]], axis=1)
            b_row = jnp.concatenate([b4[2 * p:2 * p + 1, :], b4[2 * p + 1:2 * p + 2, :]], axis=1)
            if padded:
                ok = lane_t < t_valid
                g_row = jnp.where(ok, g_row, 0.0)
                b_row = jnp.where(ok, b_row, 0.0)
            gc_row = _dot(jnp.broadcast_to(g_row, (8, PAIR)), tri_bd, HI)[0:1, :]
            gc_col = _row_to_col(gc_row, eye)
            b_col = _row_to_col(b_row, eye)
            gl_col = jnp.sum(jnp.where(last_sel, gc_row, 0.0), axis=1, keepdims=True)
            decay = jnp.where(incl, jnp.exp(jnp.where(incl, gc_col - gc_row, 0.0)), 0.0)

            lanes = slice(p * PAIR, (p + 1) * PAIR)
            rows = pl.ds(r0, c)
            q = ys_ref[rows, p * PAIR:(p + 1) * PAIR]
            k = ys_ref[rows, C_WIDTH + p * PAIR:C_WIDTH + (p + 1) * PAIR]
            v = ys_ref[rows, 2 * C_WIDTH + p * PAIR:2 * C_WIDTH + (p + 1) * PAIR]
            q_st = _stack_heads(q, lo_lane)
            k_st = _stack_heads(k, lo_lane)
            v_st = _stack_heads(v, lo_lane)
            kb_st = k_st * b_col
            lmat = jnp.where(strict, _mm(kb_st, k_st, GDN_PREC, "nt") * decay, 0.0)
            tinv = ident - lmat
            pw = lmat
            for _ in range(5):
                pw = _mm(pw, pw, GDN_INV_PREC)
                tinv = tinv + _mm(tinv, pw, GDN_INV_PREC)
            u = _mm(tinv, v_st * b_col, GDN_INV_PREC)
            wk = _mm(tinv, kb_st * jnp.exp(gc_col), GDN_INV_PREC)
            aqk = _mm(q_st, k_st, GDN_PREC, "nt") * decay
            s = s_ref[p]
            v_new = u - _mm(wk, s, GDN_PREC)
            o_st = _mm(q_st * jnp.exp(gc_col), s, GDN_PREC) + _mm(aqk, v_new, GDN_PREC)
            o_ref[pl.ds(r0, c), lanes] = o_st[:c, :] + o_st[c:, :]
            kd = k_st * jnp.exp(gl_col - gc_col)
            s_ref[p] = s * jnp.exp(gl_col) + _mm(kd, v_new, GDN_PREC, "tn")
        return carry

    lax.fori_loop(0, n_chunks, chunk, 0)

    o = o_ref[0:tc, :]
    ms = _mm(o * o, ones_hd, "x2l")
    on = o * lax.rsqrt(ms + EPS) * ng_ref[...]
    y_ref[...] = (on * jax.nn.silu(z_ref[:, qkv_w:])).astype(y_ref.dtype)

    @pl.when(ti == pl.num_programs(1) - 1)
    def _():
        st_ref[...] = s_ref[...]
        nb_ref[...] = xp_ref[CONV_PAD - 3:CONV_PAD, :]


def gdn_mixer(zc, scal, buf0, s0, cw, alog, dt, ng, ones_hd, nb, tc, t_valid, n_chunks):
    n = zc.shape[0]
    nt = n // nb // tc
    qkv_w = 3 * C_WIDTH
    full = lambda a: pl.BlockSpec(a.shape, lambda b, t: (0,) * a.ndim)
    return pl.pallas_call(
        functools.partial(_gdn_kernel, t_valid=t_valid, n_chunks=n_chunks), grid=(nb, nt),
        out_shape=[jax.ShapeDtypeStruct((n, C_WIDTH), BF16),
                   jax.ShapeDtypeStruct((nb, C_HEADS // 2, PAIR, PAIR), F32),
                   jax.ShapeDtypeStruct((nb, CONV_W - 1, qkv_w), F32)],
        in_specs=[pl.BlockSpec((tc, 4 * C_WIDTH), lambda b, t: (b * nt + t, 0)),
                  pl.BlockSpec((n_chunks, N_SMALL, GDN_CHUNK), lambda b, t: (b * nt + t, 0, 0)),
                  pl.BlockSpec((None, CONV_W - 1, qkv_w), lambda b, t: (b, 0, 0)),
                  pl.BlockSpec((None, C_HEADS // 2, PAIR, PAIR), lambda b, t: (b, 0, 0, 0)),
                  full(cw), full(alog), full(dt), full(ng), full(ones_hd)],
        out_specs=[pl.BlockSpec((tc, C_WIDTH), lambda b, t: (b * nt + t, 0)),
                   pl.BlockSpec((None, C_HEADS // 2, PAIR, PAIR), lambda b, t: (b, 0, 0, 0)),
                   pl.BlockSpec((None, CONV_W - 1, qkv_w), lambda b, t: (b, 0, 0))],
        scratch_shapes=[pltpu.VMEM((CONV_PAD + tc, qkv_w), F32),
                        pltpu.VMEM((C_HEADS // 2, PAIR, PAIR), F32),
                        pltpu.VMEM((n_chunks * GDN_CHUNK, C_WIDTH), F32),
                        pltpu.VMEM((n_chunks * GDN_CHUNK, qkv_w), F32)],
        compiler_params=_params(("parallel", "arbitrary")), name="gdn",
    )(zc, scal, buf0, s0, cw, alog, dt, ng, ones_hd)


def _outproj_kernel(h_ref, ya_ref, yb_ref, yc_ref, wo_ref, g_ref, wq_ref, hn_ref, x2_ref, q_ref):
    mix = jnp.concatenate([ya_ref[...], yb_ref[...], yc_ref[...]], axis=1)
    h = h_ref[...] + _dot(mix, wo_ref[...])
    hn_ref[...] = h
    x2 = _rms(h, g_ref[...]).astype(BF16)
    x2_ref[...] = x2
    q_ref[...] = _dot(x2, wq_ref[...])


def out_proj(h, ya, yb, yc, wo, g, wq, tm):
    n = h.shape[0]
    row = lambda w: pl.BlockSpec((tm, w), lambda i: (i, 0))
    full = lambda a: pl.BlockSpec(a.shape, lambda i: (0,) * a.ndim)
    nq = wq.shape[1]
    return pl.pallas_call(
        _outproj_kernel, grid=(n // tm,),
        out_shape=[jax.ShapeDtypeStruct((n, D_MODEL), F32), jax.ShapeDtypeStruct((n, D_MODEL), BF16),
                   jax.ShapeDtypeStruct((n, nq), F32)],
        in_specs=[row(D_MODEL), row(A_WIDTH), row(B_WIDTH), row(C_WIDTH), full(wo), full(g), full(wq)],
        out_specs=[row(D_MODEL), row(D_MODEL), row(nq)],
        compiler_params=_params(("parallel",)), name="out_proj",
    )(h, ya, yb, yc, wo, g, wq)


def _top16_rows(s, payload):
    nrow = s.shape[0]
    row = _iota(s.shape, 0)
    vals, pays = [], []
    for _ in range(P_TOPK):
        m = jnp.max(s, axis=0, keepdims=True)
        pos = jnp.min(jnp.where(s == m, row, nrow), axis=0, keepdims=True)
        hit = row == pos
        vals.append(m)
        pays.append(jnp.max(jnp.where(hit, payload, -1), axis=0, keepdims=True))
        s = jnp.where(hit, -jnp.inf, s)
    return jnp.concatenate(vals, axis=0), jnp.concatenate(pays, axis=0)


def _peer_topk_kernel(q_ref, keys_ref, eidx_ref, gate_ref):
    tn = q_ref.shape[0]
    key_row = _iota((N_KEYS, tn), 0)
    eidx, gates = [], []
    for h in range(P_HEADS):
        tops = []
        for half in range(2):
            col = (2 * h + half) * P_DK
            s = _dot_nt(keys_ref[h, half], q_ref[:, col:col + P_DK], HI)
            tops.append(_top16_rows(s, key_row))
        (s0, i0), (s1, i1) = tops
        cand = [s0[0:1, :] + s1]
        cidx = [i0[0:1, :] * N_KEYS + i1]
        b8 = _iota((8, tn), 0)
        for a in range(1, 8):
            va = s0[a:a + 1, :] + s1[0:8, :]
            nb = P_TOPK // (a + 1)
            cand.append(va if nb >= 8 else jnp.where(b8 < nb, va, -jnp.inf))
            cidx.append(i0[a:a + 1, :] * N_KEYS + i1[0:8, :])
        cand.append(s0[8:16, :] + s1[0:1, :])
        cidx.append(i0[8:16, :] * N_KEYS + i1[0:1, :])
        best, e = _top16_rows(jnp.concatenate(cand, axis=0), jnp.concatenate(cidx, axis=0))
        w = jnp.exp(best - best[0:1, :])
        gates.append(w / jnp.sum(w, axis=0, keepdims=True))
        eidx.append(e)
    eidx_ref[...] = jnp.concatenate(eidx, axis=0).T
    gate_ref[...] = jnp.concatenate(gates, axis=0).T


def peer_topk(q, keys, tn):
    n = q.shape[0]
    hk = P_HEADS * P_TOPK
    return pl.pallas_call(
        _peer_topk_kernel, grid=(n // tn,),
        out_shape=[jax.ShapeDtypeStruct((n, hk), I32), jax.ShapeDtypeStruct((n, hk), F32)],
        in_specs=[pl.BlockSpec((tn, q.shape[1]), lambda i: (i, 0)), pl.BlockSpec(keys.shape, lambda i: (0, 0, 0, 0))],
        out_specs=[pl.BlockSpec((tn, hk), lambda i: (i, 0))] * 2,
        compiler_params=_params(("parallel",)), name="peer_topk",
    )(q, keys)


E_CHUNK = 1024
SLABS = E_CHUNK // N_KEYS
KEY_BITS = N_KEYS.bit_length() - 1


N_CHUNKS = N_EXPERTS // E_CHUNK
N_PAIRS = N_CHUNKS // 2
HALF_KEYS = N_KEYS // 2
TOKEN_UNROLL = 4
U32 = jnp.uint32
HI_HALF = 0xFFFF0000


def _peer_expert_kernel(x_ref, h_ref, eidx_ref, gate_ref, ut_ref, vlo_ref, vhi_ref, o_ref, pf_ref, hm_ref):
    step = pl.program_id(1)
    tn = x_ref.shape[0]
    hk = P_HEADS * P_TOPK

    @pl.when(step == 0)
    def _():
        pf_ref[...] = jnp.zeros(pf_ref.shape, F32)

    @pl.when(step < N_CHUNKS)
    def _():
        e = eidx_ref[...]
        ii = jnp.right_shift(e, KEY_BITS)
        jj = jnp.bitwise_and(e, N_KEYS - 1)
        pc = _dot(x_ref[...], ut_ref[...])
        pf = pf_ref[...]
        for s in range(SLABS):
            got = jnp.take_along_axis(pc[:, s * N_KEYS:(s + 1) * N_KEYS], jj, axis=1)
            pf = jnp.where(ii == step * SLABS + s, got, pf)
        pf_ref[...] = pf

    @pl.when(step == N_CHUNKS - 1)
    def _():
        pf_ref[...] = _gelu_erf(pf_ref[...]) * gate_ref[...]
        sub = _iota((N_KEYS, hk), 0)

        def tokens(t, carry):
            for r in range(TOKEN_UNROLL):
                n = t * TOKEN_UNROLL + r
                e = eidx_ref[pl.ds(n, 1), :]
                hrow = pf_ref[pl.ds(n, 1), :]
                w = jnp.where(sub == jnp.right_shift(e, KEY_BITS), hrow, 0.0).astype(BF16)
                bt = jnp.where(sub == jnp.bitwise_and(e, N_KEYS - 1), 1.0, 0.0).astype(BF16)
                hn = _dot_nt(w, bt).astype(BF16).astype(F32)
                hi = pltpu.bitcast(hn[:HALF_KEYS, :], U32)
                lo = lax.shift_right_logical(pltpu.bitcast(hn[HALF_KEYS:, :], U32), U32(16))
                hm_ref[pl.ds(pl.multiple_of(n * HALF_KEYS, HALF_KEYS), HALF_KEYS), :] = hi | lo
            return carry

        lax.fori_loop(0, tn // TOKEN_UNROLL, tokens, 0)
        o_ref[...] = h_ref[...]

    @pl.when(step >= N_CHUNKS)
    def _():
        cp = step - N_CHUNKS
        words = [hm_ref[pl.ds(cp * SLABS + s, tn, stride=HALF_KEYS), :] for s in range(SLABS)]
        hi = [pltpu.bitcast(wd & U32(HI_HALF), F32).astype(BF16) for wd in words]
        lo = [pltpu.bitcast(lax.shift_left(wd, U32(16)), F32).astype(BF16) for wd in words]
        o_ref[...] += (_dot(jnp.concatenate(hi, axis=1), vlo_ref[...])
                       + _dot(jnp.concatenate(lo, axis=1), vhi_ref[...]))


def peer_experts(x2, h, eidx, gate, ut, v, tn):
    n = x2.shape[0]
    hk = P_HEADS * P_TOPK
    row = lambda w: pl.BlockSpec((tn, w), lambda i, s: (i, 0))
    pair = lambda s: jnp.clip(s - N_CHUNKS, 0, N_PAIRS - 1)
    return pl.pallas_call(
        _peer_expert_kernel, grid=(n // tn, N_CHUNKS + N_PAIRS),
        out_shape=jax.ShapeDtypeStruct((n, D_MODEL), F32),
        in_specs=[row(D_MODEL), row(D_MODEL), row(hk), row(hk),
                  pl.BlockSpec((D_MODEL, E_CHUNK), lambda i, s: (0, jnp.minimum(s, N_CHUNKS - 1))),
                  pl.BlockSpec((E_CHUNK, D_MODEL), lambda i, s: (pair(s), 0)),
                  pl.BlockSpec((E_CHUNK, D_MODEL), lambda i, s: (pair(s) + N_PAIRS, 0))],
        out_specs=row(D_MODEL),
        scratch_shapes=[pltpu.VMEM((tn, hk), F32), pltpu.VMEM((tn * HALF_KEYS, N_KEYS), U32)],
        compiler_params=_params(("parallel", "arbitrary"), 56 * 1024 * 1024), name="peer_experts",
    )(x2, h, eidx, gate, ut, v, v)


def _final_norm_kernel(h_ref, g_ref, o_ref):
    o_ref[...] = _rms(h_ref[...], g_ref[...])


def final_norm(h, g, tm):
    n = h.shape[0]
    row = pl.BlockSpec((tm, D_MODEL), lambda i: (i, 0))
    return pl.pallas_call(
        _final_norm_kernel, grid=(n // tm,), out_shape=jax.ShapeDtypeStruct(h.shape, F32),
        in_specs=[row, pl.BlockSpec(g.shape, lambda i: (0, 0))], out_specs=row,
        compiler_params=_params(("parallel",)), name="final_norm",
    )(h, g)


def _block_diag(w):
    g, a, b = w.shape
    eye = jnp.eye(g, dtype=w.dtype)
    return (eye[:, None, :, None] * w[:, :, None, :]).reshape(g * a, g * b)


def _prep_layer(W, l):
    w_in = W["w_in"][l]
    return dict(
        g1=W["norm1_g"][l][None, :],
        wa=w_in[:, OFF_AQ:OFF_AF].astype(BF16),
        wb=w_in[:, OFF_BX:OFF_CQKV].astype(BF16),
        wc=jnp.concatenate([w_in[:, OFF_CQKV:OFF_CA], w_in[:, OFF_CG:D_IN]], axis=1).astype(BF16),
        ws=jnp.concatenate([w_in[:, OFF_AF:OFF_BX], w_in[:, OFF_CA:OFF_CG]], axis=1).T,
        bf=W["b_f"][l][:, None],
        lru_cw=W["lru_conv_w"][l], lru_cb=W["lru_conv_b"][l][None, :],
        lru_wa=_block_diag(W["lru_wa"][l]), lru_ba=W["lru_ba"][l][None, :],
        lru_wx=_block_diag(W["lru_wx"][l]), lru_bx=W["lru_bx"][l][None, :],
        lru_lam=W["lru_lambda"][l][None, :],
        gdn_cw=W["gdn_conv_w"][l], gdn_alog=W["gdn_A_log"][l][:, None], gdn_dt=W["gdn_dt_bias"][l][:, None],
        gdn_ng=jnp.tile(W["gdn_norm_g"][l], C_HEADS)[None, :],
        wo=W["w_out"][l].astype(BF16), g2=W["norm2_g"][l][None, :], wq=W["peer_wq"][l].astype(BF16),
        keys=W["peer_keys"][l], ut=W["peer_u"][l].T.astype(BF16), v=W["peer_v"][l].astype(BF16),
    )


def _pair_state(s):
    nb = s.shape[0]
    s = s.reshape(nb, C_HEADS // 2, 2, C_HEAD_DIM, C_HEAD_DIM)
    eye = jnp.eye(2, dtype=s.dtype)
    return (s[:, :, :, :, None, :] * eye[None, None, :, None, :, None]).reshape(nb, C_HEADS // 2, PAIR, PAIR)


def _unpair_state(s):
    nb = s.shape[0]
    s = s.reshape(nb, C_HEADS // 2, 2, C_HEAD_DIM, 2, C_HEAD_DIM)
    return jnp.stack([s[:, :, 0, :, 0, :], s[:, :, 1, :, 1, :]], axis=2).reshape(nb, C_HEADS, C_HEAD_DIM, C_HEAD_DIM)


def _trunk(x, nb, t_real, paged, lru_h, lru_buf, gdn_s, gdn_buf, layers, final_g, cfg):
    t_rows = x.shape[1]
    n = nb * t_rows
    h = x.reshape(n, D_MODEL)
    ones_hd = _block_diag(jnp.full((C_HEADS, C_HEAD_DIM, C_HEAD_DIM), 1.0 / C_HEAD_DIM, F32))
    outs = []
    for l, lw in enumerate(layers):
        q, k, v, kb, vb, zb, zc, st = in_proj(h, lw["g1"], lw["wa"], lw["wb"], lw["wc"], lw["ws"], cfg["tm"])
        if paged is None:
            lf, c = logf_rows(st, lw["bf"], nb, True)
            ya = attn_prompt(q, kb, vb, c.reshape(A_HEADS // 2, 2, n), nb, cfg["tq"])
            n_chunks = cfg["tc_gdn"] // GDN_CHUNK
            scal = st.reshape(N_SMALL, n // GDN_CHUNK, GDN_CHUNK).transpose(1, 0, 2)
        else:
            ck, cv, clt, pt = paged
            lf, _ = logf_rows(st, lw["bf"], 1, False)
            lfn = lf.reshape(A_HEADS, nb, t_rows).transpose(1, 0, 2)
            lfn = jnp.where(jnp.arange(t_rows) < t_real, lfn, 0.0)
            lfn = jnp.pad(lfn, ((0, 0), (0, 0), (0, PAGE_SIZE - t_rows)))
            ya = attn_sample(pt, q, ck, cv, clt, k, v, lfn, l, t_real)
            n_chunks = 1
            scal = jnp.pad(st.reshape(N_SMALL, nb, t_rows).transpose(1, 0, 2),
                           ((0, 0), (0, 0), (0, GDN_CHUNK - t_rows)))
        t_valid = min(t_real, cfg["tc_lru"])
        yb, lru_ht, lru_nb = lru_mixer(zb, lru_buf[l], lru_h[l][:, None, :], lw["lru_cw"], lw["lru_cb"],
                                       lw["lru_wa"], lw["lru_ba"], lw["lru_wx"], lw["lru_bx"], lw["lru_lam"],
                                       nb, cfg["tc_lru"], t_valid)
        t_valid = min(t_real, cfg["tc_gdn"])
        yc, gdn_st, gdn_nb = gdn_mixer(zc, scal, gdn_buf[l], _pair_state(gdn_s[l]), lw["gdn_cw"], lw["gdn_alog"],
                                       lw["gdn_dt"], lw["gdn_ng"], ones_hd, nb, cfg["tc_gdn"], t_valid, n_chunks)
        hmid, x2, qp = out_proj(h, ya, yb, yc, lw["wo"], lw["g2"], lw["wq"], cfg["tm"])
        eidx, gate = peer_topk(qp, lw["keys"], cfg["tn_topk"])
        h = peer_experts(x2, hmid, eidx, gate, lw["ut"], lw["v"], cfg["tn_peer"])
        outs.append((k.reshape(nb, t_rows, A_HEADS, A_HEAD_DIM)[:, :t_real],
                     v.reshape(nb, t_rows, A_HEADS, A_HEAD_DIM)[:, :t_real],
                     lf.reshape(A_HEADS, nb, t_rows).transpose(1, 2, 0)[:, :t_real],
                     lru_ht[:, 0, :], lru_nb, _unpair_state(gdn_st), gdn_nb))
    y = final_norm(h, final_g[None, :], cfg["tm"]).reshape(nb, t_rows, D_MODEL)[:, :t_real]
    return y, tuple(jnp.stack([o[i] for o in outs], axis=0) for i in range(len(outs[0])))


PROMPT_CFG = dict(tm=512, tq=512, tc_lru=512, tc_gdn=256, tn_topk=128, tn_peer=512)
SAMPLE_CFG = dict(tm=256, tq=None, tc_lru=SAMPLE_ROWS, tc_gdn=SAMPLE_ROWS, tn_topk=128, tn_peer=256)


def kernel(x_prompt, x_sample, cache_k, cache_v, cache_logf, page_table, state_lru_h, state_lru_conv,
           state_gdn_S, state_gdn_conv, norm1_g, w_in, b_f, lru_conv_w, lru_conv_b, lru_wa, lru_ba,
           lru_wx, lru_bx, lru_lambda, gdn_conv_w, gdn_A_log, gdn_dt_bias, gdn_norm_g, w_out, norm2_g,
           peer_wq, peer_keys, peer_u, peer_v, final_norm_g):
    W = dict(norm1_g=norm1_g, w_in=w_in, b_f=b_f, lru_conv_w=lru_conv_w, lru_conv_b=lru_conv_b, lru_wa=lru_wa,
             lru_ba=lru_ba, lru_wx=lru_wx, lru_bx=lru_bx, lru_lambda=lru_lambda, gdn_conv_w=gdn_conv_w,
             gdn_A_log=gdn_A_log, gdn_dt_bias=gdn_dt_bias, gdn_norm_g=gdn_norm_g, w_out=w_out, norm2_g=norm2_g,
             peer_wq=peer_wq, peer_keys=peer_keys, peer_u=peer_u, peer_v=peer_v)
    depth = w_in.shape[0]
    layers = [_prep_layer(W, l) for l in range(depth)]
    bp, tp = x_prompt.shape[:2]
    bs, ts = x_sample.shape[:2]

    zeros = lambda *s: jnp.zeros((depth, bp) + s, F32)
    yp, sp = _trunk(x_prompt, bp, tp, None, zeros(B_WIDTH), zeros(CONV_W - 1, B_WIDTH),
                    zeros(C_HEADS, C_HEAD_DIM, C_HEAD_DIM), zeros(CONV_W - 1, 3 * C_WIDTH),
                    layers, final_norm_g, PROMPT_CFG)

    n_pool = cache_k.shape[1]
    ck = cache_k.reshape(depth, n_pool, PAGE_SIZE * A_HEADS, A_HEAD_DIM)
    cv = cache_v.reshape(depth, n_pool, PAGE_SIZE * A_HEADS, A_HEAD_DIM)
    clt = cache_logf.transpose(0, 1, 3, 2)
    xs = jnp.pad(x_sample, ((0, 0), (0, SAMPLE_ROWS - ts), (0, 0)))
    ys, ss = _trunk(xs, bs, ts, (ck, cv, clt, page_table), state_lru_h, state_lru_conv, state_gdn_S,
                    state_gdn_conv, layers, final_norm_g, SAMPLE_CFG)

    return (yp, ys, sp[0], sp[1], sp[2], ss[0], ss[1], ss[2], sp[3], ss[3], sp[4], ss[4], sp[5], ss[5], sp[6], ss[6])
```

```python
import functools

import jax
import jax.numpy as jnp
from jax import lax
from jax.experimental import pallas as pl
from jax.experimental.pallas import tpu as pltpu

F32 = jnp.float32
BF16 = jnp.bfloat16
I32 = jnp.int32
HI = lax.Precision.HIGHEST

D_MODEL = 1024
A_HEADS = 8
A_HEAD_DIM = 64
A_WIDTH = A_HEADS * A_HEAD_DIM
B_WIDTH = 256
B_BLOCKS = 4
LRU_C = 8.0
C_HEADS = 4
C_HEAD_DIM = 64
C_WIDTH = C_HEADS * C_HEAD_DIM
GDN_CHUNK = 64
CONV_W = 4
P_HEADS = 8
P_DK = 128
N_KEYS = 128
N_EXPERTS = N_KEYS * N_KEYS
P_TOPK = 16
PAGE_SIZE = 128
EPS = 1e-6
SAMPLE_ROWS = 8
N_SMALL = 16

OFF_AQ = 0
OFF_AF = OFF_AQ + 3 * A_WIDTH
OFF_BX = OFF_AF + A_HEADS
OFF_CQKV = OFF_BX + 2 * B_WIDTH
OFF_CA = OFF_CQKV + 3 * C_WIDTH
OFF_CG = OFF_CA + 2 * C_HEADS
D_IN = OFF_CG + C_WIDTH

VMEM_LIMIT = 48 * 1024 * 1024


def _params(sem, vmem=VMEM_LIMIT):
    return pltpu.CompilerParams(dimension_semantics=sem, vmem_limit_bytes=vmem)


def _dot(a, b, precision=None):
    return jnp.dot(a, b, precision=precision, preferred_element_type=F32)


def _dot_nt(a, b, precision=None):
    return lax.dot_general(a, b, (((1,), (1,)), ((), ())), precision=precision, preferred_element_type=F32)


def _dot_tn(a, b, precision=None):
    return lax.dot_general(a, b, (((0,), (0,)), ((), ())), precision=precision, preferred_element_type=F32)


_DOTS = {"nn": _dot, "nt": _dot_nt, "tn": _dot_tn}


def _split_bf16(a):
    hi = a.astype(BF16)
    return hi, (a - hi.astype(F32)).astype(BF16)


def _mm(a, b, prec, kind="nn"):
    dot = _DOTS[kind]
    if prec == "bf":
        return dot(a.astype(BF16), b.astype(BF16))
    if prec == "x2l":
        ah, al = _split_bf16(a)
        bb = b.astype(BF16)
        return dot(ah, bb) + dot(al, bb)
    if prec == "x3":
        ah, al = _split_bf16(a)
        bh, bl = _split_bf16(b)
        return dot(ah, bh) + (dot(ah, bl) + dot(al, bh))
    return dot(a, b, HI)


def _rms(x, g):
    return x * lax.rsqrt(jnp.mean(x * x, axis=-1, keepdims=True) + EPS) * g


def _gelu_erf(x):
    return 0.5 * x * (1.0 + lax.erf(x * 0.7071067811865476))


def _iota(shape, dim):
    return lax.broadcasted_iota(I32, shape, dim)


def _inproj_kernel(x_ref, g_ref, wa_ref, wb_ref, wc_ref, ws_ref,
                   q_ref, k_ref, v_ref, kb_ref, vb_ref, zb_ref, zc_ref, st_ref):
    hn = _rms(x_ref[...], g_ref[...])
    hb = hn.astype(BF16)
    za = _dot(hb, wa_ref[...])
    q_ref[...] = (za[:, :A_WIDTH] * (A_HEAD_DIM ** -0.5)).astype(BF16)
    k = za[:, A_WIDTH:2 * A_WIDTH]
    v = za[:, 2 * A_WIDTH:]
    k_ref[...] = k
    v_ref[...] = v
    kb_ref[...] = k.astype(BF16)
    vb_ref[...] = v.astype(BF16)
    zb_ref[...] = _dot(hb, wb_ref[...])
    zc_ref[...] = _dot(hb, wc_ref[...])
    st_ref[...] = _dot_nt(ws_ref[...], hn, HI)


def in_proj(h, g, wa, wb, wc, ws, tm):
    n = h.shape[0]
    grid = (n // tm,)
    row = lambda w: pl.BlockSpec((tm, w), lambda i: (i, 0))
    full = lambda a: pl.BlockSpec(a.shape, lambda i: (0,) * a.ndim)
    out_shape = [
        jax.ShapeDtypeStruct((n, A_WIDTH), BF16),
        jax.ShapeDtypeStruct((n, A_WIDTH), F32),
        jax.ShapeDtypeStruct((n, A_WIDTH), F32),
        jax.ShapeDtypeStruct((n, A_WIDTH), BF16),
        jax.ShapeDtypeStruct((n, A_WIDTH), BF16),
        jax.ShapeDtypeStruct((n, 2 * B_WIDTH), F32),
        jax.ShapeDtypeStruct((n, 4 * C_WIDTH), F32),
        jax.ShapeDtypeStruct((N_SMALL, n), F32),
    ]
    out_specs = [row(A_WIDTH)] * 5 + [row(2 * B_WIDTH), row(4 * C_WIDTH),
                                      pl.BlockSpec((N_SMALL, tm), lambda i: (0, i))]
    return pl.pallas_call(
        _inproj_kernel, grid=grid, out_shape=out_shape,
        in_specs=[row(D_MODEL), full(g), full(wa), full(wb), full(wc), full(ws)],
        out_specs=out_specs, compiler_params=_params(("parallel",)), name="in_proj",
    )(h, g, wa, wb, wc, ws)


CUMSUM_CHUNK = 512


def _logf_kernel(st_ref, bf_ref, lf_ref, c_ref, *, cumsum):
    lf = jax.nn.log_sigmoid(st_ref[0:A_HEADS, :] + bf_ref[...])
    lf_ref[...] = lf
    if not cumsum:
        c_ref[...] = lf
        return
    t = lf.shape[1]
    ch = min(CUMSUM_CHUNK, t)
    tri = (_iota((ch, ch), 0) <= _iota((ch, ch), 1)).astype(F32)
    carry = jnp.zeros((A_HEADS, 1), F32)
    for j in range(t // ch):
        cs = _dot(lf[:, j * ch:(j + 1) * ch], tri, HI) + carry
        c_ref[:, j * ch:(j + 1) * ch] = cs
        carry = cs[:, ch - 1:ch]


def logf_rows(st, bf, nb, cumsum):
    n = st.shape[1]
    t = n // nb
    spec_in = pl.BlockSpec((N_SMALL, t), lambda b: (0, b))
    spec_out = pl.BlockSpec((A_HEADS, t), lambda b: (0, b))
    return pl.pallas_call(
        functools.partial(_logf_kernel, cumsum=cumsum), grid=(nb,),
        out_shape=[jax.ShapeDtypeStruct((A_HEADS, n), F32)] * 2,
        in_specs=[spec_in, pl.BlockSpec(bf.shape, lambda b: (0, 0))],
        out_specs=[spec_out, spec_out], compiler_params=_params(("parallel",)), name="logf",
    )(st, bf)


def _attn_prompt_kernel(qi_ref, ki_ref, q_ref, k_ref, v_ref, c_ref, o_ref, m_ref, l_ref, acc_ref):
    step = pl.program_id(2)
    qi = qi_ref[step]
    ki = ki_ref[step]
    tq, tk = q_ref.shape[0], k_ref.shape[0]
    w = 2 * A_HEAD_DIM
    lane = _iota((1, w), 1)

    @pl.when(ki == 0)
    def _():
        m_ref[...] = jnp.full(m_ref.shape, -jnp.inf, F32)
        l_ref[...] = jnp.zeros(l_ref.shape, F32)
        acc_ref[...] = jnp.zeros(acc_ref.shape, F32)

    def block(masked):
        q = q_ref[...]
        k = k_ref[...]
        v = v_ref[...]
        for h in range(2):
            head = (lane >= A_HEAD_DIM) if h else (lane < A_HEAD_DIM)
            s = _dot_nt(jnp.where(head, q, jnp.zeros_like(q)), k) - c_ref[0, h:h + 1, :]
            if masked:
                s = jnp.where(_iota((tq, tk), 1) <= _iota((tq, tk), 0), s, -jnp.inf)
            m_prev = m_ref[h]
            m_new = jnp.maximum(m_prev, jnp.max(s, axis=1, keepdims=True))
            alpha = jnp.exp(m_prev - m_new)
            p = jnp.exp(s - jnp.concatenate([m_new] * (tk // w), axis=1))
            l_ref[h] = alpha * l_ref[h] + jnp.sum(p, axis=1, keepdims=True)
            acc_ref[h] = alpha * acc_ref[h] + _dot(p.astype(BF16), v)
            m_ref[h] = m_new

    @pl.when(ki < qi)
    def _():
        block(False)

    @pl.when(ki == qi)
    def _():
        block(True)
        o0 = acc_ref[0] / l_ref[0]
        o1 = acc_ref[1] / l_ref[1]
        o_ref[...] = jnp.where(lane < A_HEAD_DIM, o0, o1).astype(o_ref.dtype)


def attn_prompt(q, kb, vb, c, nb, tq):
    n = q.shape[0]
    t = n // nb
    nq = t // tq
    w = 2 * A_HEAD_DIM
    pairs = [(i, j) for i in range(nq) for j in range(i + 1)]
    qi_list = jnp.asarray([i for i, _ in pairs], I32)
    ki_list = jnp.asarray([j for _, j in pairs], I32)
    qspec = pl.BlockSpec((tq, w), lambda b, p, s, qi, ki: (b * nq + qi[s], p))
    kspec = pl.BlockSpec((tq, w), lambda b, p, s, qi, ki: (b * nq + ki[s], p))
    cspec = pl.BlockSpec((1, 2, tq), lambda b, p, s, qi, ki: (p, 0, b * nq + ki[s]))
    grid_spec = pltpu.PrefetchScalarGridSpec(
        num_scalar_prefetch=2, grid=(nb, A_HEADS // 2, len(pairs)),
        in_specs=[qspec, kspec, kspec, cspec], out_specs=qspec,
        scratch_shapes=[pltpu.VMEM((2, tq, w), F32), pltpu.VMEM((2, tq, w), F32), pltpu.VMEM((2, tq, w), F32)])
    return pl.pallas_call(
        _attn_prompt_kernel, grid_spec=grid_spec,
        out_shape=jax.ShapeDtypeStruct((n, A_WIDTH), BF16),
        compiler_params=_params(("parallel", "parallel", "arbitrary")), name="attn_prompt",
    )(qi_list, ki_list, q, kb, vb, c)


def _attn_sample_kernel(pt_ref, q_ref, ck_ref, cv_ref, cl_ref, kn_ref, vn_ref, ln_ref, o_ref,
                        qh_ref, m_ref, l_ref, acc_ref, carry_ref, *, n_pages, t_new):
    del pt_ref
    j = pl.program_id(1)
    nr = A_HEADS * SAMPLE_ROWS
    hd = A_HEAD_DIM

    @pl.when(j == 0)
    def _():
        q = q_ref[...].astype(F32)
        for h in range(A_HEADS):
            qh_ref[h] = q[:, h * hd:(h + 1) * hd]
        m_ref[...] = jnp.full(m_ref.shape, -jnp.inf, F32)
        l_ref[...] = jnp.zeros(l_ref.shape, F32)
        acc_ref[...] = jnp.zeros(acc_ref.shape, F32)
        carry_ref[...] = jnp.zeros(carry_ref.shape, F32)

    tri = (_iota((PAGE_SIZE, PAGE_SIZE), 0) <= _iota((PAGE_SIZE, PAGE_SIZE), 1)).astype(F32)

    def page(score, weighted, lf, visible):
        c = _dot(lf, tri, HI) + carry_ref[...]
        carry_ref[...] = c[:, PAGE_SIZE - 1:PAGE_SIZE]
        s = jnp.concatenate([score(h, qh_ref[h].astype(BF16)) - c[h:h + 1, :] for h in range(A_HEADS)], axis=0)
        if visible is not None:
            s = jnp.where(visible, s, -jnp.inf)
        m_prev = m_ref[...]
        m_new = jnp.maximum(m_prev, jnp.max(s, axis=1, keepdims=True))
        alpha = jnp.exp(m_prev - m_new)
        p = jnp.exp(s - m_new)
        l_ref[...] = alpha * l_ref[...] + jnp.sum(p, axis=1, keepdims=True)
        pv = jnp.concatenate([weighted(h, p[h * SAMPLE_ROWS:(h + 1) * SAMPLE_ROWS, :].astype(BF16))
                              for h in range(A_HEADS)], axis=0)
        acc_ref[...] = alpha * acc_ref[...] + pv
        m_ref[...] = m_new

    @pl.when(j < n_pages)
    def _():
        page(lambda h, qh: _dot(qh, ck_ref[h].astype(BF16)),
             lambda h, ph: _dot_nt(ph, cv_ref[h].astype(BF16)), cl_ref[...], None)

    @pl.when(j == n_pages)
    def _():
        pad = jnp.zeros((PAGE_SIZE - SAMPLE_ROWS, hd), F32)
        rows = lambda ref, h: jnp.concatenate([ref[:, h * hd:(h + 1) * hd], pad], axis=0).astype(BF16)
        visible = _iota((nr, PAGE_SIZE), 1) <= _iota((nr, PAGE_SIZE), 0) % SAMPLE_ROWS
        page(lambda h, qh: _dot_nt(qh, rows(kn_ref, h)), lambda h, ph: _dot(ph, rows(vn_ref, h)),
             ln_ref[...], visible)
        o = acc_ref[...] / l_ref[...]
        o = jnp.concatenate([o[h * SAMPLE_ROWS:(h + 1) * SAMPLE_ROWS, :] for h in range(A_HEADS)], axis=1)
        o_ref[...] = jnp.where(_iota(o.shape, 0) < t_new, o, 0.0).astype(o_ref.dtype)


def attn_sample(page_table, q, cache_k, cache_v, cache_lt, kn, vn, lfn, layer, t_new):
    nb, n_pages = page_table.shape
    nr = A_HEADS * SAMPLE_ROWS
    last = n_pages - 1
    seq = lambda w: pl.BlockSpec((SAMPLE_ROWS, w), lambda b, j, pt: (b, 0))
    page_of = lambda b, j, pt: pt[b, jnp.minimum(j, last)]
    pg = lambda r, w: pl.BlockSpec((None, None, r, w), lambda b, j, pt: (layer, page_of(b, j, pt), 0, 0))
    kv = pl.BlockSpec((None, None, A_HEADS, A_HEAD_DIM, PAGE_SIZE),
                      lambda b, j, pt: (layer, page_of(b, j, pt), 0, 0, 0))
    grid_spec = pltpu.PrefetchScalarGridSpec(
        num_scalar_prefetch=1, grid=(nb, n_pages + 1),
        in_specs=[seq(A_WIDTH), kv, kv, pg(A_HEADS, PAGE_SIZE),
                  seq(A_WIDTH), seq(A_WIDTH), pl.BlockSpec((None, A_HEADS, PAGE_SIZE), lambda b, j, pt: (b, 0, 0))],
        out_specs=seq(A_WIDTH),
        scratch_shapes=[pltpu.VMEM((A_HEADS, SAMPLE_ROWS, A_HEAD_DIM), F32), pltpu.VMEM((nr, 1), F32),
                        pltpu.VMEM((nr, 1), F32), pltpu.VMEM((nr, A_HEAD_DIM), F32), pltpu.VMEM((A_HEADS, 1), F32)])
    return pl.pallas_call(
        functools.partial(_attn_sample_kernel, n_pages=n_pages, t_new=t_new), grid_spec=grid_spec,
        out_shape=jax.ShapeDtypeStruct(q.shape, BF16),
        compiler_params=_params(("parallel", "arbitrary")), name="attn_sample",
    )(page_table, q, cache_k, cache_v, cache_lt, kn, vn, lfn)


CONV_PAD = 8


def _conv_block(xp_ref, x, w_ref, tc):
    xp_ref[CONV_PAD:CONV_PAD + tc, :] = x
    y = xp_ref[CONV_PAD - 3:CONV_PAD - 3 + tc, :] * w_ref[0:1, :]
    for i in range(1, CONV_W):
        y = y + xp_ref[CONV_PAD - 3 + i:CONV_PAD - 3 + i + tc, :] * w_ref[i:i + 1, :]
    return y


def _lru_kernel(z_ref, buf_ref, h0_ref, cw_ref, cb_ref, wa_ref, ba_ref, wx_ref, bx_ref, lam_ref,
                y_ref, ht_ref, nb_ref, xp_ref, hc_ref, *, t_valid):
    ti = pl.program_id(1)
    tc = z_ref.shape[0]

    @pl.when(ti == 0)
    def _():
        xp_ref[CONV_PAD - 3:CONV_PAD, :] = buf_ref[...]
        hc_ref[...] = h0_ref[...]

    x = z_ref[:, :B_WIDTH]
    xc = _conv_block(xp_ref, x, cw_ref, tc) + cb_ref[...]
    r = jax.nn.sigmoid(_dot(xc, wa_ref[...], HI) + ba_ref[...])
    ig = jax.nn.sigmoid(_dot(xc, wx_ref[...], HI) + bx_ref[...])
    log_a = -LRU_C * r * jax.nn.softplus(-lam_ref[...])
    a = jnp.exp(log_a)
    b = jnp.sqrt(1.0 - jnp.exp(2.0 * log_a)) * (ig * xc)
    row = _iota((tc, B_WIDTH), 0)
    d = 1
    while d < tc:
        keep = row >= d
        a_prev = jnp.where(keep, pltpu.roll(a, d, 0), 1.0)
        b_prev = jnp.where(keep, pltpu.roll(b, d, 0), 0.0)
        b = a * b_prev + b
        a = a * a_prev
        d *= 2
    hs = a * hc_ref[...] + b
    y_ref[...] = (hs * jax.nn.gelu(z_ref[:, B_WIDTH:])).astype(y_ref.dtype)
    hc_ref[...] = hs[t_valid - 1:t_valid, :]
    xp_ref[CONV_PAD - 3:CONV_PAD, :] = xp_ref[CONV_PAD + t_valid - 3:CONV_PAD + t_valid, :]

    @pl.when(ti == pl.num_programs(1) - 1)
    def _():
        ht_ref[...] = hc_ref[...]
        nb_ref[...] = xp_ref[CONV_PAD - 3:CONV_PAD, :]


def lru_mixer(zb, buf0, h0, cw, cb, wa, ba, wx, bx, lam, nb, tc, t_valid):
    n = zb.shape[0]
    nt = n // nb // tc
    full = lambda a: pl.BlockSpec(a.shape, lambda b, t: (0,) * a.ndim)
    per_b = lambda r: pl.BlockSpec((None, r, B_WIDTH), lambda b, t: (b, 0, 0))
    return pl.pallas_call(
        functools.partial(_lru_kernel, t_valid=t_valid), grid=(nb, nt),
        out_shape=[jax.ShapeDtypeStruct((n, B_WIDTH), BF16), jax.ShapeDtypeStruct((nb, 1, B_WIDTH), F32),
                   jax.ShapeDtypeStruct((nb, CONV_W - 1, B_WIDTH), F32)],
        in_specs=[pl.BlockSpec((tc, 2 * B_WIDTH), lambda b, t: (b * nt + t, 0)), per_b(CONV_W - 1), per_b(1),
                  full(cw), full(cb), full(wa), full(ba), full(wx), full(bx), full(lam)],
        out_specs=[pl.BlockSpec((tc, B_WIDTH), lambda b, t: (b * nt + t, 0)), per_b(1), per_b(CONV_W - 1)],
        scratch_shapes=[pltpu.VMEM((CONV_PAD + tc, B_WIDTH), F32), pltpu.VMEM((1, B_WIDTH), F32)],
        compiler_params=_params(("parallel", "arbitrary")), name="lru",
    )(zb, buf0, h0, cw, cb, wa, ba, wx, bx, lam)


PAIR = 2 * C_HEAD_DIM
GDN_INV_PREC = "bf"
GDN_PREC = "bf"


def _stack_heads(x, lo_mask):
    return jnp.concatenate([jnp.where(lo_mask, x, 0.0), jnp.where(lo_mask, 0.0, x)], axis=0)


def _row_to_col(row, eye):
    return jnp.sum(jnp.where(eye, row, 0.0), axis=1, keepdims=True)


def _gdn_kernel(z_ref, sc_ref, buf_ref, s0_ref, cw_ref, alog_ref, dt_ref, ng_ref, ones_hd_ref,
                y_ref, st_ref, nb_ref, xp_ref, s_ref, o_ref, ys_ref, wy_ref, eg_ref, *, t_valid, n_chunks):
    ti = pl.program_id(1)
    tc = z_ref.shape[0]
    c = GDN_CHUNK
    qkv_w = 3 * C_WIDTH

    @pl.when(ti == 0)
    def _():
        xp_ref[CONV_PAD - 3:CONV_PAD, :] = buf_ref[...]
        s_ref[...] = s0_ref[...]

    y = jax.nn.silu(_conv_block(xp_ref, z_ref[:, :qkv_w], cw_ref, tc))
    xp_ref[CONV_PAD - 3:CONV_PAD, :] = xp_ref[CONV_PAD + t_valid - 3:CONV_PAD + t_valid, :]
    padded = tc < n_chunks * c
    if padded:
        y = jnp.concatenate([y, jnp.zeros((n_chunks * c - tc, qkv_w), F32)], axis=0)
    ones_hd = ones_hd_ref[...]

    def l2n(x):
        return x * lax.rsqrt(_mm(x * x, ones_hd, "x2l") * C_HEAD_DIM + EPS)

    ys_ref[:, :C_WIDTH] = l2n(y[:, :C_WIDTH]) * (C_HEAD_DIM ** -0.5)
    ys_ref[:, C_WIDTH:2 * C_WIDTH] = l2n(y[:, C_WIDTH:2 * C_WIDTH])
    ys_ref[:, 2 * C_WIDTH:] = y[:, 2 * C_WIDTH:]

    r128 = _iota((PAIR, PAIR), 0)
    c128 = _iota((PAIR, PAIR), 1)
    same = (r128 // c) == (c128 // c)
    eye = r128 == c128
    incl = same & (c128 <= r128)
    strict = same & (c128 < r128)
    tri_bd = (same & (r128 <= c128)).astype(F32)
    last_sel = same & (c128 % c == c - 1)
    lo_lane = _iota((1, PAIR), 1) < c
    lane_t = _iota((1, PAIR), 1) % c
    ident = eye.astype(F32)

    for ci in range(n_chunks):
        r0 = ci * c
        sc = sc_ref[ci]
        g4 = -jnp.exp(alog_ref[...]) * jax.nn.softplus(sc[8:12, :] + dt_ref[...])
        b4 = jax.nn.sigmoid(sc[12:16, :])
        for p in range(C_HEADS // 2):
            g_row = jnp.concatenate([g4[2 * p:2 * p + 1, :], g4[2 * p + 1:2 * p + 2, :]], axis=1)
            b_row = jnp.concatenate([b4[2 * p:2 * p + 1, :], b4[2 * p + 1:2 * p + 2, :]], axis=1)
            if padded:
                ok = lane_t < t_valid
                g_row = jnp.where(ok, g_row, 0.0)
                b_row = jnp.where(ok, b_row, 0.0)
            gc_row = _dot(jnp.broadcast_to(g_row, (8, PAIR)), tri_bd, HI)[0:1, :]
            gc_col = _row_to_col(gc_row, eye)
            b_col = _row_to_col(b_row, eye)
            gl_col = jnp.sum(jnp.where(last_sel, gc_row, 0.0), axis=1, keepdims=True)
            decay = jnp.where(incl, jnp.exp(jnp.where(incl, gc_col - gc_row, 0.0)), 0.0)

            rows = slice(r0, r0 + c)
            q = ys_ref[rows, p * PAIR:(p + 1) * PAIR]
            k = ys_ref[rows, C_WIDTH + p * PAIR:C_WIDTH + (p + 1) * PAIR]
            v = ys_ref[rows, 2 * C_WIDTH + p * PAIR:2 * C_WIDTH + (p + 1) * PAIR]
            q_st = _stack_heads(q, lo_lane)
            k_st = _stack_heads(k, lo_lane)
            v_st = _stack_heads(v, lo_lane)
            kb_st = k_st * b_col
            lmat = jnp.where(strict, _mm(kb_st, k_st, GDN_PREC, "nt") * decay, 0.0)
            tinv = ident - lmat
            pw = lmat
            for _ in range(5):
                pw = _mm(pw, pw, GDN_INV_PREC)
                tinv = tinv + _mm(tinv, pw, GDN_INV_PREC)
            wy_ref[0, ci, p] = _mm(tinv, v_st * b_col, GDN_INV_PREC)
            wy_ref[1, ci, p] = _mm(tinv, kb_st * jnp.exp(gc_col), GDN_INV_PREC)
            wy_ref[2, ci, p] = _mm(q_st, k_st, GDN_PREC, "nt") * decay
            wy_ref[3, ci, p] = q_st * jnp.exp(gc_col)
            wy_ref[4, ci, p] = k_st * jnp.exp(gl_col - gc_col)
            eg_ref[ci, p] = jnp.exp(gl_col)

    for ci in range(n_chunks):
        for p in range(C_HEADS // 2):
            s = s_ref[p]
            v_new = wy_ref[0, ci, p] - _mm(wy_ref[1, ci, p], s, GDN_PREC)
            o_st = _mm(wy_ref[3, ci, p], s, GDN_PREC) + _mm(wy_ref[2, ci, p], v_new, GDN_PREC)
            o_ref[ci * c:(ci + 1) * c, p * PAIR:(p + 1) * PAIR] = o_st[:c, :] + o_st[c:, :]
            s_ref[p] = s * eg_ref[ci, p] + _mm(wy_ref[4, ci, p], v_new, GDN_PREC, "tn")

    o = o_ref[0:tc, :]
    ms = _mm(o * o, ones_hd, "x2l")
    on = o * lax.rsqrt(ms + EPS) * ng_ref[...]
    y_ref[...] = (on * jax.nn.silu(z_ref[:, qkv_w:])).astype(y_ref.dtype)

    @pl.when(ti == pl.num_programs(1) - 1)
    def _():
        st_ref[...] = s_ref[...]
        nb_ref[...] = xp_ref[CONV_PAD - 3:CONV_PAD, :]


def gdn_mixer(zc, scal, buf0, s0, cw, alog, dt, ng, ones_hd, nb, tc, t_valid, n_chunks):
    n = zc.shape[0]
    nt = n // nb // tc
    qkv_w = 3 * C_WIDTH
    full = lambda a: pl.BlockSpec(a.shape, lambda b, t: (0,) * a.ndim)
    return pl.pallas_call(
        functools.partial(_gdn_kernel, t_valid=t_valid, n_chunks=n_chunks), grid=(nb, nt),
        out_shape=[jax.ShapeDtypeStruct((n, C_WIDTH), BF16),
                   jax.ShapeDtypeStruct((nb, C_HEADS // 2, PAIR, PAIR), F32),
                   jax.ShapeDtypeStruct((nb, CONV_W - 1, qkv_w), F32)],
        in_specs=[pl.BlockSpec((tc, 4 * C_WIDTH), lambda b, t: (b * nt + t, 0)),
                  pl.BlockSpec((n_chunks, N_SMALL, GDN_CHUNK), lambda b, t: (b * nt + t, 0, 0)),
                  pl.BlockSpec((None, CONV_W - 1, qkv_w), lambda b, t: (b, 0, 0)),
                  pl.BlockSpec((None, C_HEADS // 2, PAIR, PAIR), lambda b, t: (b, 0, 0, 0)),
                  full(cw), full(alog), full(dt), full(ng), full(ones_hd)],
        out_specs=[pl.BlockSpec((tc, C_WIDTH), lambda b, t: (b * nt + t, 0)),
                   pl.BlockSpec((None, C_HEADS // 2, PAIR, PAIR), lambda b, t: (b, 0, 0, 0)),
                   pl.BlockSpec((None, CONV_W - 1, qkv_w), lambda b, t: (b, 0, 0))],
        scratch_shapes=[pltpu.VMEM((CONV_PAD + tc, qkv_w), F32),
                        pltpu.VMEM((C_HEADS // 2, PAIR, PAIR), F32),
                        pltpu.VMEM((n_chunks * GDN_CHUNK, C_WIDTH), F32),
                        pltpu.VMEM((n_chunks * GDN_CHUNK, qkv_w), F32),
                        pltpu.VMEM((5, n_chunks, C_HEADS // 2, PAIR, PAIR), F32),
                        pltpu.VMEM((n_chunks, C_HEADS // 2, PAIR, 1), F32)],
        compiler_params=_params(("parallel", "arbitrary")), name="gdn",
    )(zc, scal, buf0, s0, cw, alog, dt, ng, ones_hd)


def _outproj_kernel(h_ref, ya_ref, yb_ref, yc_ref, wo_ref, g_ref, wq_ref, hn_ref, x2_ref, q_ref):
    mix = jnp.concatenate([ya_ref[...], yb_ref[...], yc_ref[...]], axis=1)
    h = h_ref[...] + _dot(mix, wo_ref[...])
    hn_ref[...] = h
    x2 = _rms(h, g_ref[...]).astype(BF16)
    x2_ref[...] = x2
    q_ref[...] = _dot(x2, wq_ref[...])


def out_proj(h, ya, yb, yc, wo, g, wq, tm):
    n = h.shape[0]
    row = lambda w: pl.BlockSpec((tm, w), lambda i: (i, 0))
    full = lambda a: pl.BlockSpec(a.shape, lambda i: (0,) * a.ndim)
    nq = wq.shape[1]
    return pl.pallas_call(
        _outproj_kernel, grid=(n // tm,),
        out_shape=[jax.ShapeDtypeStruct((n, D_MODEL), F32), jax.ShapeDtypeStruct((n, D_MODEL), BF16),
                   jax.ShapeDtypeStruct((n, nq), F32)],
        in_specs=[row(D_MODEL), row(A_WIDTH), row(B_WIDTH), row(C_WIDTH), full(wo), full(g), full(wq)],
        out_specs=[row(D_MODEL), row(D_MODEL), row(nq)],
        compiler_params=_params(("parallel",)), name="out_proj",
    )(h, ya, yb, yc, wo, g, wq)


def _top16_rows(s, payload=None):
    nrow = s.shape[0]
    row = _iota(s.shape, 0)
    vals, pays = [], []
    for _ in range(P_TOPK):
        m = jnp.max(s, axis=0, keepdims=True)
        pos = jnp.min(jnp.where(s == m, row, nrow), axis=0, keepdims=True)
        hit = row == pos
        vals.append(m)
        pays.append(pos if payload is None else jnp.max(jnp.where(hit, payload, -1), axis=0, keepdims=True))
        s = jnp.where(hit, -jnp.inf, s)
    return jnp.concatenate(vals, axis=0), jnp.concatenate(pays, axis=0)


def _peer_topk_kernel(q_ref, keys_ref, eidx_ref, gate_ref):
    tn = q_ref.shape[0]
    eidx, gates = [], []
    for h in range(P_HEADS):
        tops = []
        for half in range(2):
            col = (2 * h + half) * P_DK
            s = _dot_nt(keys_ref[h, half], q_ref[:, col:col + P_DK], HI)
            tops.append(_top16_rows(s))
        (s0, i0), (s1, i1) = tops
        cand = [s0[0:1, :] + s1]
        cidx = [i0[0:1, :] * N_KEYS + i1]
        b8 = _iota((8, tn), 0)
        for a in range(1, 8):
            va = s0[a:a + 1, :] + s1[0:8, :]
            nb = P_TOPK // (a + 1)
            cand.append(va if nb >= 8 else jnp.where(b8 < nb, va, -jnp.inf))
            cidx.append(i0[a:a + 1, :] * N_KEYS + i1[0:8, :])
        cand.append(s0[8:16, :] + s1[0:1, :])
        cidx.append(i0[8:16, :] * N_KEYS + i1[0:1, :])
        best, e = _top16_rows(jnp.concatenate(cand, axis=0), jnp.concatenate(cidx, axis=0))
        w = jnp.exp(best - best[0:1, :])
        gates.append(w / jnp.sum(w, axis=0, keepdims=True))
        eidx.append(e)
    eidx_ref[...] = jnp.concatenate(eidx, axis=0).T
    gate_ref[...] = jnp.concatenate(gates, axis=0).T


def peer_topk(q, keys, tn):
    n = q.shape[0]
    hk = P_HEADS * P_TOPK
    return pl.pallas_call(
        _peer_topk_kernel, grid=(n // tn,),
        out_shape=[jax.ShapeDtypeStruct((n, hk), I32), jax.ShapeDtypeStruct((n, hk), F32)],
        in_specs=[pl.BlockSpec((tn, q.shape[1]), lambda i: (i, 0)), pl.BlockSpec(keys.shape, lambda i: (0, 0, 0, 0))],
        out_specs=[pl.BlockSpec((tn, hk), lambda i: (i, 0))] * 2,
        compiler_params=_params(("parallel",)), name="peer_topk",
    )(q, keys)


E_CHUNK = 1024
SLABS = E_CHUNK // N_KEYS
KEY_BITS = N_KEYS.bit_length() - 1


N_CHUNKS = N_EXPERTS // E_CHUNK
N_PAIRS = N_CHUNKS // 2
HALF_KEYS = N_KEYS // 2
TOKEN_UNROLL = 8
U32 = jnp.uint32
HI_HALF = 0xFFFF0000


def _peer_expert_kernel(x_ref, h_ref, eidx_ref, gate_ref, ut_ref, vlo_ref, vhi_ref, o_ref, pf_ref, hm_ref):
    step = pl.program_id(1)
    tn = x_ref.shape[0]
    hk = P_HEADS * P_TOPK

    @pl.when(step == 0)
    def _():
        pf_ref[...] = jnp.zeros(pf_ref.shape, F32)

    @pl.when(step < N_CHUNKS)
    def _():
        e = eidx_ref[...]
        ii = jnp.right_shift(e, KEY_BITS)
        jj = jnp.bitwise_and(e, N_KEYS - 1)
        pc = _dot(x_ref[...], ut_ref[...])
        pf = pf_ref[...]
        for s in range(SLABS):
            got = jnp.take_along_axis(pc[:, s * N_KEYS:(s + 1) * N_KEYS], jj, axis=1)
            pf = jnp.where(ii == step * SLABS + s, got, pf)
        pf_ref[...] = pf

    @pl.when(step == N_CHUNKS - 1)
    def _():
        pf_ref[...] = _gelu_erf(pf_ref[...]) * gate_ref[...]
        sub = _iota((N_KEYS, hk), 0)

        def tokens(t, carry):
            for r in range(TOKEN_UNROLL):
                n = t * TOKEN_UNROLL + r
                e = eidx_ref[pl.ds(n, 1), :]
                hrow = pf_ref[pl.ds(n, 1), :]
                w = jnp.where(sub == jnp.right_shift(e, KEY_BITS), hrow, 0.0).astype(BF16)
                bt = jnp.where(sub == jnp.bitwise_and(e, N_KEYS - 1), 1.0, 0.0).astype(BF16)
                hn = _dot_nt(w, bt).astype(BF16).astype(F32)
                hi = pltpu.bitcast(hn[:HALF_KEYS, :], U32)
                lo = lax.shift_right_logical(pltpu.bitcast(hn[HALF_KEYS:, :], U32), U32(16))
                hm_ref[pl.ds(pl.multiple_of(n * HALF_KEYS, HALF_KEYS), HALF_KEYS), :] = hi | lo
            return carry

        lax.fori_loop(0, tn // TOKEN_UNROLL, tokens, 0)
        o_ref[...] = h_ref[...]

    @pl.when(step >= N_CHUNKS)
    def _():
        cp = step - N_CHUNKS
        words = [hm_ref[pl.ds(cp * SLABS + s, tn, stride=HALF_KEYS), :] for s in range(SLABS)]
        hi = [pltpu.bitcast(wd & U32(HI_HALF), F32).astype(BF16) for wd in words]
        lo = [pltpu.bitcast(lax.shift_left(wd, U32(16)), F32).astype(BF16) for wd in words]
        o_ref[...] += (_dot(jnp.concatenate(hi, axis=1), vlo_ref[...])
                       + _dot(jnp.concatenate(lo, axis=1), vhi_ref[...]))


def peer_experts(x2, h, eidx, gate, ut, v, tn):
    n = x2.shape[0]
    hk = P_HEADS * P_TOPK
    row = lambda w: pl.BlockSpec((tn, w), lambda i, s: (i, 0))
    pair = lambda s: jnp.clip(s - N_CHUNKS, 0, N_PAIRS - 1)
    return pl.pallas_call(
        _peer_expert_kernel, grid=(n // tn, N_CHUNKS + N_PAIRS),
        out_shape=jax.ShapeDtypeStruct((n, D_MODEL), F32),
        in_specs=[row(D_MODEL), row(D_MODEL), row(hk), row(hk),
                  pl.BlockSpec((D_MODEL, E_CHUNK), lambda i, s: (0, jnp.minimum(s, N_CHUNKS - 1))),
                  pl.BlockSpec((E_CHUNK, D_MODEL), lambda i, s: (pair(s), 0)),
                  pl.BlockSpec((E_CHUNK, D_MODEL), lambda i, s: (pair(s) + N_PAIRS, 0))],
        out_specs=row(D_MODEL),
        scratch_shapes=[pltpu.VMEM((tn, hk), F32), pltpu.VMEM((tn * HALF_KEYS, N_KEYS), U32)],
        compiler_params=_params(("parallel", "arbitrary"), 56 * 1024 * 1024), name="peer_experts",
    )(x2, h, eidx, gate, ut, v, v)


def _final_norm_kernel(h_ref, g_ref, o_ref):
    o_ref[...] = _rms(h_ref[...], g_ref[...])


def final_norm(h, g, tm):
    n = h.shape[0]
    row = pl.BlockSpec((tm, D_MODEL), lambda i: (i, 0))
    return pl.pallas_call(
        _final_norm_kernel, grid=(n // tm,), out_shape=jax.ShapeDtypeStruct(h.shape, F32),
        in_specs=[row, pl.BlockSpec(g.shape, lambda i: (0, 0))], out_specs=row,
        compiler_params=_params(("parallel",)), name="final_norm",
    )(h, g)


def _block_diag(w):
    g, a, b = w.shape
    eye = jnp.eye(g, dtype=w.dtype)
    return (eye[:, None, :, None] * w[:, :, None, :]).reshape(g * a, g * b)


def _prep_layer(W, l):
    w_in = W["w_in"][l]
    return dict(
        g1=W["norm1_g"][l][None, :],
        wa=w_in[:, OFF_AQ:OFF_AF].astype(BF16),
        wb=w_in[:, OFF_BX:OFF_CQKV].astype(BF16),
        wc=jnp.concatenate([w_in[:, OFF_CQKV:OFF_CA], w_in[:, OFF_CG:D_IN]], axis=1).astype(BF16),
        ws=jnp.concatenate([w_in[:, OFF_AF:OFF_BX], w_in[:, OFF_CA:OFF_CG]], axis=1).T,
        bf=W["b_f"][l][:, None],
        lru_cw=W["lru_conv_w"][l], lru_cb=W["lru_conv_b"][l][None, :],
        lru_wa=_block_diag(W["lru_wa"][l]), lru_ba=W["lru_ba"][l][None, :],
        lru_wx=_block_diag(W["lru_wx"][l]), lru_bx=W["lru_bx"][l][None, :],
        lru_lam=W["lru_lambda"][l][None, :],
        gdn_cw=W["gdn_conv_w"][l], gdn_alog=W["gdn_A_log"][l][:, None], gdn_dt=W["gdn_dt_bias"][l][:, None],
        gdn_ng=jnp.tile(W["gdn_norm_g"][l], C_HEADS)[None, :],
        wo=W["w_out"][l].astype(BF16), g2=W["norm2_g"][l][None, :], wq=W["peer_wq"][l].astype(BF16),
        keys=W["peer_keys"][l], ut=W["peer_u"][l].T.astype(BF16), v=W["peer_v"][l].astype(BF16),
    )


def _pair_state(s):
    nb = s.shape[0]
    s = s.reshape(nb, C_HEADS // 2, 2, C_HEAD_DIM, C_HEAD_DIM)
    eye = jnp.eye(2, dtype=s.dtype)
    return (s[:, :, :, :, None, :] * eye[None, None, :, None, :, None]).reshape(nb, C_HEADS // 2, PAIR, PAIR)


def _unpair_state(s):
    nb = s.shape[0]
    s = s.reshape(nb, C_HEADS // 2, 2, C_HEAD_DIM, 2, C_HEAD_DIM)
    return jnp.stack([s[:, :, 0, :, 0, :], s[:, :, 1, :, 1, :]], axis=2).reshape(nb, C_HEADS, C_HEAD_DIM, C_HEAD_DIM)


def _trunk(x, nb, t_real, paged, lru_h, lru_buf, gdn_s, gdn_buf, layers, final_g, cfg):
    t_rows = x.shape[1]
    n = nb * t_rows
    h = x.reshape(n, D_MODEL)
    ones_hd = _block_diag(jnp.full((C_HEADS, C_HEAD_DIM, C_HEAD_DIM), 1.0 / C_HEAD_DIM, F32))
    outs = []
    for l, lw in enumerate(layers):
        q, k, v, kb, vb, zb, zc, st = in_proj(h, lw["g1"], lw["wa"], lw["wb"], lw["wc"], lw["ws"], cfg["tm"])
        if paged is None:
            lf, c = logf_rows(st, lw["bf"], nb, True)
            ya = attn_prompt(q, kb, vb, c.reshape(A_HEADS // 2, 2, n), nb, cfg["tq"])
            n_chunks = cfg["tc_gdn"] // GDN_CHUNK
            scal = st.reshape(N_SMALL, n // GDN_CHUNK, GDN_CHUNK).transpose(1, 0, 2)
        else:
            ck, cv, clt, pt = paged
            lf, _ = logf_rows(st, lw["bf"], 1, False)
            lfn = lf.reshape(A_HEADS, nb, t_rows).transpose(1, 0, 2)
            lfn = jnp.where(jnp.arange(t_rows) < t_real, lfn, 0.0)
            lfn = jnp.pad(lfn, ((0, 0), (0, 0), (0, PAGE_SIZE - t_rows)))
            ya = attn_sample(pt, q, ck, cv, clt, k, v, lfn, l, t_real)
            n_chunks = 1
            scal = jnp.pad(st.reshape(N_SMALL, nb, t_rows).transpose(1, 0, 2),
                           ((0, 0), (0, 0), (0, GDN_CHUNK - t_rows)))
        t_valid = min(t_real, cfg["tc_lru"])
        yb, lru_ht, lru_nb = lru_mixer(zb, lru_buf[l], lru_h[l][:, None, :], lw["lru_cw"], lw["lru_cb"],
                                       lw["lru_wa"], lw["lru_ba"], lw["lru_wx"], lw["lru_bx"], lw["lru_lam"],
                                       nb, cfg["tc_lru"], t_valid)
        t_valid = min(t_real, cfg["tc_gdn"])
        yc, gdn_st, gdn_nb = gdn_mixer(zc, scal, gdn_buf[l], _pair_state(gdn_s[l]), lw["gdn_cw"], lw["gdn_alog"],
                                       lw["gdn_dt"], lw["gdn_ng"], ones_hd, nb, cfg["tc_gdn"], t_valid, n_chunks)
        hmid, x2, qp = out_proj(h, ya, yb, yc, lw["wo"], lw["g2"], lw["wq"], cfg["tm"])
        eidx, gate = peer_topk(qp, lw["keys"], cfg["tn_topk"])
        h = peer_experts(x2, hmid, eidx, gate, lw["ut"], lw["v"], cfg["tn_peer"])
        outs.append((k.reshape(nb, t_rows, A_HEADS, A_HEAD_DIM)[:, :t_real],
                     v.reshape(nb, t_rows, A_HEADS, A_HEAD_DIM)[:, :t_real],
                     lf.reshape(A_HEADS, nb, t_rows).transpose(1, 2, 0)[:, :t_real],
                     lru_ht[:, 0, :], lru_nb, _unpair_state(gdn_st), gdn_nb))
    y = final_norm(h, final_g[None, :], cfg["tm"]).reshape(nb, t_rows, D_MODEL)[:, :t_real]
    return y, tuple(jnp.stack([o[i] for o in outs], axis=0) for i in range(len(outs[0])))


PROMPT_CFG = dict(tm=512, tq=512, tc_lru=512, tc_gdn=256, tn_topk=128, tn_peer=512)
SAMPLE_CFG = dict(tm=256, tq=None, tc_lru=SAMPLE_ROWS, tc_gdn=SAMPLE_ROWS, tn_topk=128, tn_peer=256)


def kernel(x_prompt, x_sample, cache_k, cache_v, cache_logf, page_table, state_lru_h, state_lru_conv,
           state_gdn_S, state_gdn_conv, norm1_g, w_in, b_f, lru_conv_w, lru_conv_b, lru_wa, lru_ba,
           lru_wx, lru_bx, lru_lambda, gdn_conv_w, gdn_A_log, gdn_dt_bias, gdn_norm_g, w_out, norm2_g,
           peer_wq, peer_keys, peer_u, peer_v, final_norm_g):
    W = dict(norm1_g=norm1_g, w_in=w_in, b_f=b_f, lru_conv_w=lru_conv_w, lru_conv_b=lru_conv_b, lru_wa=lru_wa,
             lru_ba=lru_ba, lru_wx=lru_wx, lru_bx=lru_bx, lru_lambda=lru_lambda, gdn_conv_w=gdn_conv_w,
             gdn_A_log=gdn_A_log, gdn_dt_bias=gdn_dt_bias, gdn_norm_g=gdn_norm_g, w_out=w_out, norm2_g=norm2_g,
             peer_wq=peer_wq, peer_keys=peer_keys, peer_u=peer_u, peer_v=peer_v)
    depth = w_in.shape[0]
    layers = [_prep_layer(W, l) for l in range(depth)]
    bp, tp = x_prompt.shape[:2]
    bs, ts = x_sample.shape[:2]

    zeros = lambda *s: jnp.zeros((depth, bp) + s, F32)
    yp, sp = _trunk(x_prompt, bp, tp, None, zeros(B_WIDTH), zeros(CONV_W - 1, B_WIDTH),
                    zeros(C_HEADS, C_HEAD_DIM, C_HEAD_DIM), zeros(CONV_W - 1, 3 * C_WIDTH),
                    layers, final_norm_g, PROMPT_CFG)

    n_pool = cache_k.shape[1]
    ck = cache_k.transpose(0, 1, 3, 4, 2)
    cv = cache_v.transpose(0, 1, 3, 4, 2)
    clt = cache_logf.transpose(0, 1, 3, 2)
    xs = jnp.pad(x_sample, ((0, 0), (0, SAMPLE_ROWS - ts), (0, 0)))
    ys, ss = _trunk(xs, bs, ts, (ck, cv, clt, page_table), state_lru_h, state_lru_conv, state_gdn_S,
                    state_gdn_conv, layers, final_norm_g, SAMPLE_CFG)

    return (yp, ys, sp[0], sp[1], sp[2], ss[0], ss[1], ss[2], sp[3], ss[3], sp[4], ss[4], sp[5], ss[5], sp[6], ss[6])
```

```python
import functools

import jax
import jax.numpy as jnp
from jax import lax
from jax.experimental import pallas as pl
from jax.experimental.pallas import tpu as pltpu

F32 = jnp.float32
BF16 = jnp.bfloat16
I32 = jnp.int32
HI = lax.Precision.HIGHEST

D_MODEL = 1024
A_HEADS = 8
A_HEAD_DIM = 64
A_WIDTH = A_HEADS * A_HEAD_DIM
B_WIDTH = 256
B_BLOCKS = 4
LRU_C = 8.0
C_HEADS = 4
C_HEAD_DIM = 64
C_WIDTH = C_HEADS * C_HEAD_DIM
GDN_CHUNK = 64
CONV_W = 4
P_HEADS = 8
P_DK = 128
N_KEYS = 128
N_EXPERTS = N_KEYS * N_KEYS
P_TOPK = 16
PAGE_SIZE = 128
EPS = 1e-6
SAMPLE_ROWS = 8
N_SMALL = 16

OFF_AQ = 0
OFF_AF = OFF_AQ + 3 * A_WIDTH
OFF_BX = OFF_AF + A_HEADS
OFF_CQKV = OFF_BX + 2 * B_WIDTH
OFF_CA = OFF_CQKV + 3 * C_WIDTH
OFF_CG = OFF_CA + 2 * C_HEADS
D_IN = OFF_CG + C_WIDTH

VMEM_LIMIT = 48 * 1024 * 1024


def _params(sem, vmem=VMEM_LIMIT):
    return pltpu.CompilerParams(dimension_semantics=sem, vmem_limit_bytes=vmem)


def _dot(a, b, precision=None):
    return jnp.dot(a, b, precision=precision, preferred_element_type=F32)


def _dot_nt(a, b, precision=None):
    return lax.dot_general(a, b, (((1,), (1,)), ((), ())), precision=precision, preferred_element_type=F32)


def _dot_tn(a, b, precision=None):
    return lax.dot_general(a, b, (((0,), (0,)), ((), ())), precision=precision, preferred_element_type=F32)


_DOTS = {"nn": _dot, "nt": _dot_nt, "tn": _dot_tn}


def _split_bf16(a):
    hi = a.astype(BF16)
    return hi, (a - hi.astype(F32)).astype(BF16)


def _mm(a, b, prec, kind="nn"):
    dot = _DOTS[kind]
    if prec == "bf":
        return dot(a.astype(BF16), b.astype(BF16))
    if prec == "x2l":
        ah, al = _split_bf16(a)
        bb = b.astype(BF16)
        return dot(ah, bb) + dot(al, bb)
    if prec == "x3":
        ah, al = _split_bf16(a)
        bh, bl = _split_bf16(b)
        return dot(ah, bh) + (dot(ah, bl) + dot(al, bh))
    return dot(a, b, HI)


def _rms(x, g):
    return x * lax.rsqrt(jnp.mean(x * x, axis=-1, keepdims=True) + EPS) * g


def _gelu_erf(x):
    return 0.5 * x * (1.0 + lax.erf(x * 0.7071067811865476))


def _iota(shape, dim):
    return lax.broadcasted_iota(I32, shape, dim)


def _inproj_kernel(x_ref, g_ref, wa_ref, wb_ref, wc_ref, ws_ref,
                   q_ref, k_ref, v_ref, kb_ref, vb_ref, zb_ref, zc_ref, st_ref):
    hn = _rms(x_ref[...], g_ref[...])
    hb = hn.astype(BF16)
    za = _dot(hb, wa_ref[...])
    q_ref[...] = (za[:, :A_WIDTH] * (A_HEAD_DIM ** -0.5)).astype(BF16)
    k = za[:, A_WIDTH:2 * A_WIDTH]
    v = za[:, 2 * A_WIDTH:]
    k_ref[...] = k
    v_ref[...] = v
    kb_ref[...] = k.astype(BF16)
    vb_ref[...] = v.astype(BF16)
    zb_ref[...] = _dot(hb, wb_ref[...])
    zc_ref[...] = _dot(hb, wc_ref[...])
    st_ref[...] = _dot_nt(ws_ref[...], hn, HI)


def in_proj(h, g, wa, wb, wc, ws, tm):
    n = h.shape[0]
    grid = (n // tm,)
    row = lambda w: pl.BlockSpec((tm, w), lambda i: (i, 0))
    full = lambda a: pl.BlockSpec(a.shape, lambda i: (0,) * a.ndim)
    out_shape = [
        jax.ShapeDtypeStruct((n, A_WIDTH), BF16),
        jax.ShapeDtypeStruct((n, A_WIDTH), F32),
        jax.ShapeDtypeStruct((n, A_WIDTH), F32),
        jax.ShapeDtypeStruct((n, A_WIDTH), BF16),
        jax.ShapeDtypeStruct((n, A_WIDTH), BF16),
        jax.ShapeDtypeStruct((n, 2 * B_WIDTH), F32),
        jax.ShapeDtypeStruct((n, 4 * C_WIDTH), F32),
        jax.ShapeDtypeStruct((N_SMALL, n), F32),
    ]
    out_specs = [row(A_WIDTH)] * 5 + [row(2 * B_WIDTH), row(4 * C_WIDTH),
                                      pl.BlockSpec((N_SMALL, tm), lambda i: (0, i))]
    return pl.pallas_call(
        _inproj_kernel, grid=grid, out_shape=out_shape,
        in_specs=[row(D_MODEL), full(g), full(wa), full(wb), full(wc), full(ws)],
        out_specs=out_specs, compiler_params=_params(("parallel",)), name="in_proj",
    )(h, g, wa, wb, wc, ws)


CUMSUM_CHUNK = 512


def _logf_kernel(st_ref, bf_ref, lf_ref, c_ref, *, cumsum):
    lf = jax.nn.log_sigmoid(st_ref[0:A_HEADS, :] + bf_ref[...])
    lf_ref[...] = lf
    if not cumsum:
        c_ref[...] = lf
        return
    t = lf.shape[1]
    ch = min(CUMSUM_CHUNK, t)
    tri = (_iota((ch, ch), 0) <= _iota((ch, ch), 1)).astype(F32)
    carry = jnp.zeros((A_HEADS, 1), F32)
    for j in range(t // ch):
        cs = _dot(lf[:, j * ch:(j + 1) * ch], tri, HI) + carry
        c_ref[:, j * ch:(j + 1) * ch] = cs
        carry = cs[:, ch - 1:ch]


def logf_rows(st, bf, nb, cumsum):
    n = st.shape[1]
    t = n // nb
    spec_in = pl.BlockSpec((N_SMALL, t), lambda b: (0, b))
    spec_out = pl.BlockSpec((A_HEADS, t), lambda b: (0, b))
    return pl.pallas_call(
        functools.partial(_logf_kernel, cumsum=cumsum), grid=(nb,),
        out_shape=[jax.ShapeDtypeStruct((A_HEADS, n), F32)] * 2,
        in_specs=[spec_in, pl.BlockSpec(bf.shape, lambda b: (0, 0))],
        out_specs=[spec_out, spec_out], compiler_params=_params(("parallel",)), name="logf",
    )(st, bf)


def _attn_prompt_kernel(qi_ref, ki_ref, q_ref, k_ref, v_ref, c_ref, o_ref, m_ref, l_ref, acc_ref):
    step = pl.program_id(2)
    qi = qi_ref[step]
    ki = ki_ref[step]
    tq, tk = q_ref.shape[0], k_ref.shape[0]
    w = 2 * A_HEAD_DIM
    lane = _iota((1, w), 1)

    @pl.when(ki == 0)
    def _():
        m_ref[...] = jnp.full(m_ref.shape, -jnp.inf, F32)
        l_ref[...] = jnp.zeros(l_ref.shape, F32)
        acc_ref[...] = jnp.zeros(acc_ref.shape, F32)

    def block(masked):
        q = q_ref[...]
        k = k_ref[...]
        v = v_ref[...]
        for h in range(2):
            head = (lane >= A_HEAD_DIM) if h else (lane < A_HEAD_DIM)
            s = _dot_nt(jnp.where(head, q, jnp.zeros_like(q)), k) - c_ref[0, h:h + 1, :]
            if masked:
                s = jnp.where(_iota((tq, tk), 1) <= _iota((tq, tk), 0), s, -jnp.inf)
            m_prev = m_ref[h]
            m_new = jnp.maximum(m_prev, jnp.max(s, axis=1, keepdims=True))
            alpha = jnp.exp(m_prev - m_new)
            p = jnp.exp(s - jnp.concatenate([m_new] * (tk // w), axis=1))
            l_ref[h] = alpha * l_ref[h] + jnp.sum(p, axis=1, keepdims=True)
            acc_ref[h] = alpha * acc_ref[h] + _dot(p.astype(BF16), v)
            m_ref[h] = m_new

    @pl.when(ki < qi)
    def _():
        block(False)

    @pl.when(ki == qi)
    def _():
        block(True)
        o0 = acc_ref[0] / l_ref[0]
        o1 = acc_ref[1] / l_ref[1]
        o_ref[...] = jnp.where(lane < A_HEAD_DIM, o0, o1).astype(o_ref.dtype)


def attn_prompt(q, kb, vb, c, nb, tq):
    n = q.shape[0]
    t = n // nb
    nq = t // tq
    w = 2 * A_HEAD_DIM
    pairs = [(i, j) for i in range(nq) for j in range(i + 1)]
    qi_list = jnp.asarray([i for i, _ in pairs], I32)
    ki_list = jnp.asarray([j for _, j in pairs], I32)
    qspec = pl.BlockSpec((tq, w), lambda b, p, s, qi, ki: (b * nq + qi[s], p))
    kspec = pl.BlockSpec((tq, w), lambda b, p, s, qi, ki: (b * nq + ki[s], p))
    cspec = pl.BlockSpec((1, 2, tq), lambda b, p, s, qi, ki: (p, 0, b * nq + ki[s]))
    grid_spec = pltpu.PrefetchScalarGridSpec(
        num_scalar_prefetch=2, grid=(nb, A_HEADS // 2, len(pairs)),
        in_specs=[qspec, kspec, kspec, cspec], out_specs=qspec,
        scratch_shapes=[pltpu.VMEM((2, tq, w), F32), pltpu.VMEM((2, tq, w), F32), pltpu.VMEM((2, tq, w), F32)])
    return pl.pallas_call(
        _attn_prompt_kernel, grid_spec=grid_spec,
        out_shape=jax.ShapeDtypeStruct((n, A_WIDTH), BF16),
        compiler_params=_params(("parallel", "parallel", "arbitrary")), name="attn_prompt",
    )(qi_list, ki_list, q, kb, vb, c)


PAGE_GROUP = 8


def _attn_sample_kernel(pt_ref, q_ref, kn_ref, vn_ref, ln_ref, ck_hbm, cv_hbm, cl_hbm, o_ref,
                        kbuf, vbuf, lbuf, c_ref, qh_ref, m_ref, l_ref, acc_ref, ksem, vsem, lsem,
                        *, layer, n_pages, t_new):
    b = pl.program_id(0)
    nr = A_HEADS * SAMPLE_ROWS
    hd = A_HEAD_DIM
    n_groups = n_pages // PAGE_GROUP

    def kv_copies(g, slot):
        cps = []
        for i in range(PAGE_GROUP):
            pg = pt_ref[b, g * PAGE_GROUP + i]
            cps.append(pltpu.make_async_copy(ck_hbm.at[layer, pg], kbuf.at[slot, i], ksem.at[slot]))
            cps.append(pltpu.make_async_copy(cv_hbm.at[layer, pg], vbuf.at[slot, i], vsem.at[slot]))
        return cps

    def lf_copy(j):
        return pltpu.make_async_copy(cl_hbm.at[layer, pt_ref[b, j]], lbuf.at[j], lsem)

    for j in range(n_pages):
        lf_copy(j).start()
    for cp in kv_copies(0, 0):
        cp.start()

    q = q_ref[...].astype(F32)
    for h in range(A_HEADS):
        qh_ref[h] = q[:, h * hd:(h + 1) * hd]
    m_ref[...] = jnp.full(m_ref.shape, -jnp.inf, F32)
    l_ref[...] = jnp.zeros(l_ref.shape, F32)
    acc_ref[...] = jnp.zeros(acc_ref.shape, F32)

    for j in range(n_pages):
        lf_copy(j).wait()
    tri = (_iota((PAGE_SIZE, PAGE_SIZE), 0) <= _iota((PAGE_SIZE, PAGE_SIZE), 1)).astype(F32)
    nrow = n_pages * A_HEADS
    cs = _dot(lbuf[...].reshape(nrow, PAGE_SIZE), tri, HI)
    totals = jnp.broadcast_to(cs[:, PAGE_SIZE - 1:PAGE_SIZE], (nrow, PAGE_SIZE))
    r, cc = _iota((nrow, nrow), 0), _iota((nrow, nrow), 1)
    earlier = ((cc // A_HEADS < r // A_HEADS) & (cc % A_HEADS == r % A_HEADS)).astype(F32)
    c_all = cs + _dot(earlier, totals, HI)
    c_ref[...] = c_all.reshape(n_pages, A_HEADS, PAGE_SIZE)
    carry = c_all[nrow - A_HEADS:, PAGE_SIZE - 1:PAGE_SIZE]

    def update(s, weighted):
        m_prev = m_ref[...]
        m_new = jnp.maximum(m_prev, jnp.max(s, axis=1, keepdims=True))
        alpha = jnp.exp(m_prev - m_new)
        p = jnp.exp(s - m_new)
        l_ref[...] = alpha * l_ref[...] + jnp.sum(p, axis=1, keepdims=True)
        pv = jnp.concatenate([weighted(h, p[h * SAMPLE_ROWS:(h + 1) * SAMPLE_ROWS, :].astype(BF16))
                              for h in range(A_HEADS)], axis=0)
        acc_ref[...] = alpha * acc_ref[...] + pv
        m_ref[...] = m_new

    def group(g, carry_):
        slot = g % 2

        @pl.when(g + 1 < n_groups)
        def _():
            for cp in kv_copies(g + 1, 1 - slot):
                cp.start()

        for cp in kv_copies(g, slot):
            cp.wait()
        kb = kbuf.at[slot]
        vb = vbuf.at[slot]
        tiles = []
        for i in range(PAGE_GROUP):
            c = c_ref[g * PAGE_GROUP + i]
            tiles.append(jnp.concatenate(
                [_dot(qh_ref[h].astype(BF16), kb[i, h].astype(BF16)) - c[h:h + 1, :] for h in range(A_HEADS)], axis=0))

        def weighted(h, ph):
            out = _dot_nt(ph[:, 0:PAGE_SIZE], vb[0, h].astype(BF16))
            for i in range(1, PAGE_GROUP):
                out = out + _dot_nt(ph[:, i * PAGE_SIZE:(i + 1) * PAGE_SIZE], vb[i, h].astype(BF16))
            return out

        update(jnp.concatenate(tiles, axis=1), weighted)
        return carry_

    lax.fori_loop(0, n_groups, group, 0)

    pad = jnp.zeros((PAGE_SIZE - SAMPLE_ROWS, hd), F32)
    rows = lambda ref, h: jnp.concatenate([ref[:, h * hd:(h + 1) * hd], pad], axis=0).astype(BF16)
    c_new = _dot(ln_ref[...], tri, HI) + carry
    visible = _iota((nr, PAGE_SIZE), 1) <= _iota((nr, PAGE_SIZE), 0) % SAMPLE_ROWS
    s_new = jnp.concatenate([_dot_nt(qh_ref[h].astype(BF16), rows(kn_ref, h)) - c_new[h:h + 1, :]
                             for h in range(A_HEADS)], axis=0)
    update(jnp.where(visible, s_new, -jnp.inf), lambda h, ph: _dot(ph, rows(vn_ref, h)))

    o = acc_ref[...] / l_ref[...]
    o = jnp.concatenate([o[h * SAMPLE_ROWS:(h + 1) * SAMPLE_ROWS, :] for h in range(A_HEADS)], axis=1)
    o_ref[...] = jnp.where(_iota(o.shape, 0) < t_new, o, 0.0).astype(o_ref.dtype)


def attn_sample(page_table, q, cache_k, cache_v, cache_lt, kn, vn, lfn, layer, t_new):
    nb, n_pages = page_table.shape
    assert n_pages % PAGE_GROUP == 0
    nr = A_HEADS * SAMPLE_ROWS
    seq = lambda w: pl.BlockSpec((SAMPLE_ROWS, w), lambda b, pt: (b, 0))
    hbm = pl.BlockSpec(memory_space=pl.ANY)
    page_buf = pltpu.VMEM((2, PAGE_GROUP, A_HEADS, A_HEAD_DIM, PAGE_SIZE), F32)
    grid_spec = pltpu.PrefetchScalarGridSpec(
        num_scalar_prefetch=1, grid=(nb,),
        in_specs=[seq(A_WIDTH), seq(A_WIDTH), seq(A_WIDTH),
                  pl.BlockSpec((None, A_HEADS, PAGE_SIZE), lambda b, pt: (b, 0, 0)), hbm, hbm, hbm],
        out_specs=seq(A_WIDTH),
        scratch_shapes=[page_buf, page_buf,
                        pltpu.VMEM((n_pages, A_HEADS, PAGE_SIZE), F32), pltpu.VMEM((n_pages, A_HEADS, PAGE_SIZE), F32),
                        pltpu.VMEM((A_HEADS, SAMPLE_ROWS, A_HEAD_DIM), F32), pltpu.VMEM((nr, 1), F32),
                        pltpu.VMEM((nr, 1), F32), pltpu.VMEM((nr, A_HEAD_DIM), F32),
                        pltpu.SemaphoreType.DMA((2,)), pltpu.SemaphoreType.DMA((2,)), pltpu.SemaphoreType.DMA(())])
    return pl.pallas_call(
        functools.partial(_attn_sample_kernel, layer=layer, n_pages=n_pages, t_new=t_new), grid_spec=grid_spec,
        out_shape=jax.ShapeDtypeStruct(q.shape, BF16),
        compiler_params=_params(("arbitrary",)), name="attn_sample",
    )(page_table, q, kn, vn, lfn, cache_k, cache_v, cache_lt)


CONV_PAD = 8


def _conv_block(xp_ref, x, w_ref, tc):
    xp_ref[CONV_PAD:CONV_PAD + tc, :] = x
    y = xp_ref[CONV_PAD - 3:CONV_PAD - 3 + tc, :] * w_ref[0:1, :]
    for i in range(1, CONV_W):
        y = y + xp_ref[CONV_PAD - 3 + i:CONV_PAD - 3 + i + tc, :] * w_ref[i:i + 1, :]
    return y


def _lru_kernel(z_ref, buf_ref, h0_ref, cw_ref, cb_ref, wa_ref, ba_ref, wx_ref, bx_ref, lam_ref,
                y_ref, ht_ref, nb_ref, xp_ref, hc_ref, *, t_valid):
    ti = pl.program_id(1)
    tc = z_ref.shape[0]

    @pl.when(ti == 0)
    def _():
        xp_ref[CONV_PAD - 3:CONV_PAD, :] = buf_ref[...]
        hc_ref[...] = h0_ref[...]

    x = z_ref[:, :B_WIDTH]
    xc = _conv_block(xp_ref, x, cw_ref, tc) + cb_ref[...]
    r = jax.nn.sigmoid(_dot(xc, wa_ref[...], HI) + ba_ref[...])
    ig = jax.nn.sigmoid(_dot(xc, wx_ref[...], HI) + bx_ref[...])
    log_a = -LRU_C * r * jax.nn.softplus(-lam_ref[...])
    a = jnp.exp(log_a)
    b = jnp.sqrt(1.0 - jnp.exp(2.0 * log_a)) * (ig * xc)
    row = _iota((tc, B_WIDTH), 0)
    d = 1
    while d < tc:
        keep = row >= d
        a_prev = jnp.where(keep, pltpu.roll(a, d, 0), 1.0)
        b_prev = jnp.where(keep, pltpu.roll(b, d, 0), 0.0)
        b = a * b_prev + b
        a = a * a_prev
        d *= 2
    hs = a * hc_ref[...] + b
    y_ref[...] = (hs * jax.nn.gelu(z_ref[:, B_WIDTH:])).astype(y_ref.dtype)
    hc_ref[...] = hs[t_valid - 1:t_valid, :]
    xp_ref[CONV_PAD - 3:CONV_PAD, :] = xp_ref[CONV_PAD + t_valid - 3:CONV_PAD + t_valid, :]

    @pl.when(ti == pl.num_programs(1) - 1)
    def _():
        ht_ref[...] = hc_ref[...]
        nb_ref[...] = xp_ref[CONV_PAD - 3:CONV_PAD, :]


def lru_mixer(zb, buf0, h0, cw, cb, wa, ba, wx, bx, lam, nb, tc, t_valid):
    n = zb.shape[0]
    nt = n // nb // tc
    full = lambda a: pl.BlockSpec(a.shape, lambda b, t: (0,) * a.ndim)
    per_b = lambda r: pl.BlockSpec((None, r, B_WIDTH), lambda b, t: (b, 0, 0))
    return pl.pallas_call(
        functools.partial(_lru_kernel, t_valid=t_valid), grid=(nb, nt),
        out_shape=[jax.ShapeDtypeStruct((n, B_WIDTH), BF16), jax.ShapeDtypeStruct((nb, 1, B_WIDTH), F32),
                   jax.ShapeDtypeStruct((nb, CONV_W - 1, B_WIDTH), F32)],
        in_specs=[pl.BlockSpec((tc, 2 * B_WIDTH), lambda b, t: (b * nt + t, 0)), per_b(CONV_W - 1), per_b(1),
                  full(cw), full(cb), full(wa), full(ba), full(wx), full(bx), full(lam)],
        out_specs=[pl.BlockSpec((tc, B_WIDTH), lambda b, t: (b * nt + t, 0)), per_b(1), per_b(CONV_W - 1)],
        scratch_shapes=[pltpu.VMEM((CONV_PAD + tc, B_WIDTH), F32), pltpu.VMEM((1, B_WIDTH), F32)],
        compiler_params=_params(("parallel", "arbitrary")), name="lru",
    )(zb, buf0, h0, cw, cb, wa, ba, wx, bx, lam)


PAIR = 2 * C_HEAD_DIM
GDN_INV_PREC = "bf"
GDN_PREC = "bf"
GDN_LOCKSTEP = 4


def _stack_heads(x, lo_mask):
    return jnp.concatenate([jnp.where(lo_mask, x, 0.0), jnp.where(lo_mask, 0.0, x)], axis=0)


def _row_to_col(row, eye):
    return jnp.sum(jnp.where(eye, row, 0.0), axis=1, keepdims=True)


def _gdn_kernel(z_ref, sc_ref, buf_ref, s0_ref, cw_ref, alog_ref, dt_ref, ng_ref, ones_hd_ref,
                y_ref, st_ref, nb_ref, xp_ref, s_ref, o_ref, ys_ref, wy_ref, eg_ref, *, t_valid, n_chunks):
    ti = pl.program_id(1)
    tc = z_ref.shape[0]
    c = GDN_CHUNK
    qkv_w = 3 * C_WIDTH

    @pl.when(ti == 0)
    def _():
        xp_ref[CONV_PAD - 3:CONV_PAD, :] = buf_ref[...]
        s_ref[...] = s0_ref[...]

    y = jax.nn.silu(_conv_block(xp_ref, z_ref[:, :qkv_w], cw_ref, tc))
    xp_ref[CONV_PAD - 3:CONV_PAD, :] = xp_ref[CONV_PAD + t_valid - 3:CONV_PAD + t_valid, :]
    padded = tc < n_chunks * c
    if padded:
        y = jnp.concatenate([y, jnp.zeros((n_chunks * c - tc, qkv_w), F32)], axis=0)
    ones_hd = ones_hd_ref[...]

    def l2n(x):
        return x * lax.rsqrt(_mm(x * x, ones_hd, "x2l") * C_HEAD_DIM + EPS)

    ys_ref[:, :C_WIDTH] = l2n(y[:, :C_WIDTH]) * (C_HEAD_DIM ** -0.5)
    ys_ref[:, C_WIDTH:2 * C_WIDTH] = l2n(y[:, C_WIDTH:2 * C_WIDTH])
    ys_ref[:, 2 * C_WIDTH:] = y[:, 2 * C_WIDTH:]

    r128 = _iota((PAIR, PAIR), 0)
    c128 = _iota((PAIR, PAIR), 1)
    same = (r128 // c) == (c128 // c)
    eye = r128 == c128
    incl = same & (c128 <= r128)
    strict = same & (c128 < r128)
    tri_bd = (same & (r128 <= c128)).astype(F32)
    last_sel = same & (c128 % c == c - 1)
    lo_lane = _iota((1, PAIR), 1) < c
    lane_t = _iota((1, PAIR), 1) % c
    ident = eye.astype(F32)

    def prepare(ci, p):
        sc = sc_ref[ci]
        g4 = -jnp.exp(alog_ref[...]) * jax.nn.softplus(sc[8:12, :] + dt_ref[...])
        b4 = jax.nn.sigmoid(sc[12:16, :])
        g_row = jnp.concatenate([g4[2 * p:2 * p + 1, :], g4[2 * p + 1:2 * p + 2, :]], axis=1)
        b_row = jnp.concatenate([b4[2 * p:2 * p + 1, :], b4[2 * p + 1:2 * p + 2, :]], axis=1)
        if padded:
            ok = lane_t < t_valid
            g_row = jnp.where(ok, g_row, 0.0)
            b_row = jnp.where(ok, b_row, 0.0)
        gc_row = _dot(jnp.broadcast_to(g_row, (8, PAIR)), tri_bd, HI)[0:1, :]
        gc_col = _row_to_col(gc_row, eye)
        b_col = _row_to_col(b_row, eye)
        gl_col = jnp.sum(jnp.where(last_sel, gc_row, 0.0), axis=1, keepdims=True)
        decay = jnp.where(incl, jnp.exp(jnp.where(incl, gc_col - gc_row, 0.0)), 0.0)
        rows = slice(ci * c, (ci + 1) * c)
        q_st = _stack_heads(ys_ref[rows, p * PAIR:(p + 1) * PAIR], lo_lane)
        k_st = _stack_heads(ys_ref[rows, C_WIDTH + p * PAIR:C_WIDTH + (p + 1) * PAIR], lo_lane)
        v_st = _stack_heads(ys_ref[rows, 2 * C_WIDTH + p * PAIR:2 * C_WIDTH + (p + 1) * PAIR], lo_lane)
        kb_st = k_st * b_col
        wy_ref[3, ci, p] = q_st * jnp.exp(gc_col)
        wy_ref[4, ci, p] = k_st * jnp.exp(gl_col - gc_col)
        eg_ref[ci, p] = jnp.exp(gl_col)
        return dict(decay=decay, q=q_st, k=k_st, kb=kb_st, vb=v_st * b_col, kbe=kb_st * jnp.exp(gc_col))

    chains = [(ci, p) for ci in range(n_chunks) for p in range(C_HEADS // 2)]
    for g0 in range(0, len(chains), GDN_LOCKSTEP):
        group = chains[g0:g0 + GDN_LOCKSTEP]
        st = [prepare(ci, p) for ci, p in group]
        lmat = [jnp.where(strict, _mm(d["kb"], d["k"], GDN_PREC, "nt") * d["decay"], 0.0) for d in st]
        tinv = [ident - m for m in lmat]
        pw = lmat
        for _ in range(5):
            pw = [_mm(x, x, GDN_INV_PREC) for x in pw]
            tinv = [t + _mm(t, x, GDN_INV_PREC) for t, x in zip(tinv, pw)]
        for (ci, p), d, t in zip(group, st, tinv):
            wy_ref[0, ci, p] = _mm(t, d["vb"], GDN_INV_PREC)
            wy_ref[1, ci, p] = _mm(t, d["kbe"], GDN_INV_PREC)
            wy_ref[2, ci, p] = _mm(d["q"], d["k"], GDN_PREC, "nt") * d["decay"]

    for ci in range(n_chunks):
        for p in range(C_HEADS // 2):
            s = s_ref[p]
            v_new = wy_ref[0, ci, p] - _mm(wy_ref[1, ci, p], s, GDN_PREC)
            o_st = _mm(wy_ref[3, ci, p], s, GDN_PREC) + _mm(wy_ref[2, ci, p], v_new, GDN_PREC)
            o_ref[ci * c:(ci + 1) * c, p * PAIR:(p + 1) * PAIR] = o_st[:c, :] + o_st[c:, :]
            s_ref[p] = s * eg_ref[ci, p] + _mm(wy_ref[4, ci, p], v_new, GDN_PREC, "tn")

    o = o_ref[0:tc, :]
    ms = _mm(o * o, ones_hd, "x2l")
    on = o * lax.rsqrt(ms + EPS) * ng_ref[...]
    y_ref[...] = (on * jax.nn.silu(z_ref[:, qkv_w:])).astype(y_ref.dtype)

    @pl.when(ti == pl.num_programs(1) - 1)
    def _():
        st_ref[...] = s_ref[...]
        nb_ref[...] = xp_ref[CONV_PAD - 3:CONV_PAD, :]


def gdn_mixer(zc, scal, buf0, s0, cw, alog, dt, ng, ones_hd, nb, tc, t_valid, n_chunks):
    n = zc.shape[0]
    nt = n // nb // tc
    qkv_w = 3 * C_WIDTH
    full = lambda a: pl.BlockSpec(a.shape, lambda b, t: (0,) * a.ndim)
    return pl.pallas_call(
        functools.partial(_gdn_kernel, t_valid=t_valid, n_chunks=n_chunks), grid=(nb, nt),
        out_shape=[jax.ShapeDtypeStruct((n, C_WIDTH), BF16),
                   jax.ShapeDtypeStruct((nb, C_HEADS // 2, PAIR, PAIR), F32),
                   jax.ShapeDtypeStruct((nb, CONV_W - 1, qkv_w), F32)],
        in_specs=[pl.BlockSpec((tc, 4 * C_WIDTH), lambda b, t: (b * nt + t, 0)),
                  pl.BlockSpec((n_chunks, N_SMALL, GDN_CHUNK), lambda b, t: (b * nt + t, 0, 0)),
                  pl.BlockSpec((None, CONV_W - 1, qkv_w), lambda b, t: (b, 0, 0)),
                  pl.BlockSpec((None, C_HEADS // 2, PAIR, PAIR), lambda b, t: (b, 0, 0, 0)),
                  full(cw), full(alog), full(dt), full(ng), full(ones_hd)],
        out_specs=[pl.BlockSpec((tc, C_WIDTH), lambda b, t: (b * nt + t, 0)),
                   pl.BlockSpec((None, C_HEADS // 2, PAIR, PAIR), lambda b, t: (b, 0, 0, 0)),
                   pl.BlockSpec((None, CONV_W - 1, qkv_w), lambda b, t: (b, 0, 0))],
        scratch_shapes=[pltpu.VMEM((CONV_PAD + tc, qkv_w), F32),
                        pltpu.VMEM((C_HEADS // 2, PAIR, PAIR), F32),
                        pltpu.VMEM((n_chunks * GDN_CHUNK, C_WIDTH), F32),
                        pltpu.VMEM((n_chunks * GDN_CHUNK, qkv_w), F32),
                        pltpu.VMEM((5, n_chunks, C_HEADS // 2, PAIR, PAIR), F32),
                        pltpu.VMEM((n_chunks, C_HEADS // 2, PAIR, 1), F32)],
        compiler_params=_params(("parallel", "arbitrary")), name="gdn",
    )(zc, scal, buf0, s0, cw, alog, dt, ng, ones_hd)


def _outproj_kernel(h_ref, ya_ref, yb_ref, yc_ref, wo_ref, g_ref, wq_ref, hn_ref, x2_ref, q_ref):
    mix = jnp.concatenate([ya_ref[...], yb_ref[...], yc_ref[...]], axis=1)
    h = h_ref[...] + _dot(mix, wo_ref[...])
    hn_ref[...] = h
    x2 = _rms(h, g_ref[...]).astype(BF16)
    x2_ref[...] = x2
    q_ref[...] = _dot(x2, wq_ref[...])


def out_proj(h, ya, yb, yc, wo, g, wq, tm):
    n = h.shape[0]
    row = lambda w: pl.BlockSpec((tm, w), lambda i: (i, 0))
    full = lambda a: pl.BlockSpec(a.shape, lambda i: (0,) * a.ndim)
    nq = wq.shape[1]
    return pl.pallas_call(
        _outproj_kernel, grid=(n // tm,),
        out_shape=[jax.ShapeDtypeStruct((n, D_MODEL), F32), jax.ShapeDtypeStruct((n, D_MODEL), BF16),
                   jax.ShapeDtypeStruct((n, nq), F32)],
        in_specs=[row(D_MODEL), row(A_WIDTH), row(B_WIDTH), row(C_WIDTH), full(wo), full(g), full(wq)],
        out_specs=[row(D_MODEL), row(D_MODEL), row(nq)],
        compiler_params=_params(("parallel",)), name="out_proj",
    )(h, ya, yb, yc, wo, g, wq)


def _top16_rows(s, payload=None):
    nrow = s.shape[0]
    row = _iota(s.shape, 0)
    vals, pays = [], []
    for _ in range(P_TOPK):
        m = jnp.max(s, axis=0, keepdims=True)
        pos = jnp.min(jnp.where(s == m, row, nrow), axis=0, keepdims=True)
        hit = row == pos
        vals.append(m)
        pays.append(pos if payload is None else jnp.max(jnp.where(hit, payload, -1), axis=0, keepdims=True))
        s = jnp.where(hit, -jnp.inf, s)
    return jnp.concatenate(vals, axis=0), jnp.concatenate(pays, axis=0)


def _peer_topk_kernel(q_ref, keys_ref, eidx_ref, gate_ref):
    tn = q_ref.shape[0]
    eidx, gates = [], []
    for h in range(P_HEADS):
        tops = []
        for half in range(2):
            col = (2 * h + half) * P_DK
            s = _dot_nt(keys_ref[h, half], q_ref[:, col:col + P_DK], HI)
            tops.append(_top16_rows(s))
        (s0, i0), (s1, i1) = tops
        cand = [s0[0:1, :] + s1]
        cidx = [i0[0:1, :] * N_KEYS + i1]
        b8 = _iota((8, tn), 0)
        for a in range(1, 8):
            va = s0[a:a + 1, :] + s1[0:8, :]
            nb = P_TOPK // (a + 1)
            cand.append(va if nb >= 8 else jnp.where(b8 < nb, va, -jnp.inf))
            cidx.append(i0[a:a + 1, :] * N_KEYS + i1[0:8, :])
        cand.append(s0[8:16, :] + s1[0:1, :])
        cidx.append(i0[8:16, :] * N_KEYS + i1[0:1, :])
        best, e = _top16_rows(jnp.concatenate(cand, axis=0), jnp.concatenate(cidx, axis=0))
        w = jnp.exp(best - best[0:1, :])
        gates.append(w / jnp.sum(w, axis=0, keepdims=True))
        eidx.append(e)
    eidx_ref[...] = jnp.concatenate(eidx, axis=0).T
    gate_ref[...] = jnp.concatenate(gates, axis=0).T


def peer_topk(q, keys, tn):
    n = q.shape[0]
    hk = P_HEADS * P_TOPK
    return pl.pallas_call(
        _peer_topk_kernel, grid=(n // tn,),
        out_shape=[jax.ShapeDtypeStruct((n, hk), I32), jax.ShapeDtypeStruct((n, hk), F32)],
        in_specs=[pl.BlockSpec((tn, q.shape[1]), lambda i: (i, 0)), pl.BlockSpec(keys.shape, lambda i: (0, 0, 0, 0))],
        out_specs=[pl.BlockSpec((tn, hk), lambda i: (i, 0))] * 2,
        compiler_params=_params(("parallel",)), name="peer_topk",
    )(q, keys)


E_CHUNK = 1024
SLABS = E_CHUNK // N_KEYS
KEY_BITS = N_KEYS.bit_length() - 1


N_CHUNKS = N_EXPERTS // E_CHUNK
N_PAIRS = N_CHUNKS // 2
HALF_KEYS = N_KEYS // 2
TOKEN_UNROLL = 8
U32 = jnp.uint32
HI_HALF = 0xFFFF0000


def _peer_expert_kernel(x_ref, h_ref, eidx_ref, gate_ref, ut_ref, vlo_ref, vhi_ref, o_ref, pf_ref, hm_ref):
    step = pl.program_id(1)
    tn = x_ref.shape[0]
    hk = P_HEADS * P_TOPK

    @pl.when(step == 0)
    def _():
        pf_ref[...] = jnp.zeros(pf_ref.shape, F32)

    @pl.when(step < N_CHUNKS)
    def _():
        e = eidx_ref[...]
        ii = jnp.right_shift(e, KEY_BITS)
        jj = jnp.bitwise_and(e, N_KEYS - 1)
        pc = _dot(x_ref[...], ut_ref[...])
        pf = pf_ref[...]
        for s in range(SLABS):
            got = jnp.take_along_axis(pc[:, s * N_KEYS:(s + 1) * N_KEYS], jj, axis=1)
            pf = jnp.where(ii == step * SLABS + s, got, pf)
        pf_ref[...] = pf

    @pl.when(step == N_CHUNKS - 1)
    def _():
        pf_ref[...] = _gelu_erf(pf_ref[...]) * gate_ref[...]
        sub = _iota((N_KEYS, hk), 0)

        def tokens(t, carry):
            for r in range(TOKEN_UNROLL):
                n = t * TOKEN_UNROLL + r
                e = eidx_ref[pl.ds(n, 1), :]
                hrow = pf_ref[pl.ds(n, 1), :]
                w = jnp.where(sub == jnp.right_shift(e, KEY_BITS), hrow, 0.0).astype(BF16)
                bt = jnp.where(sub == jnp.bitwise_and(e, N_KEYS - 1), 1.0, 0.0).astype(BF16)
                hn = _dot_nt(w, bt).astype(BF16).astype(F32)
                hi = pltpu.bitcast(hn[:HALF_KEYS, :], U32)
                lo = lax.shift_right_logical(pltpu.bitcast(hn[HALF_KEYS:, :], U32), U32(16))
                hm_ref[pl.ds(pl.multiple_of(n * HALF_KEYS, HALF_KEYS), HALF_KEYS), :] = hi | lo
            return carry

        lax.fori_loop(0, tn // TOKEN_UNROLL, tokens, 0)
        o_ref[...] = h_ref[...]

    @pl.when(step >= N_CHUNKS)
    def _():
        cp = step - N_CHUNKS
        words = [hm_ref[pl.ds(cp * SLABS + s, tn, stride=HALF_KEYS), :] for s in range(SLABS)]
        hi = [pltpu.bitcast(wd & U32(HI_HALF), F32).astype(BF16) for wd in words]
        lo = [pltpu.bitcast(lax.shift_left(wd, U32(16)), F32).astype(BF16) for wd in words]
        o_ref[...] += (_dot(jnp.concatenate(hi, axis=1), vlo_ref[...])
                       + _dot(jnp.concatenate(lo, axis=1), vhi_ref[...]))


def peer_experts(x2, h, eidx, gate, ut, v, tn):
    n = x2.shape[0]
    hk = P_HEADS * P_TOPK
    row = lambda w: pl.BlockSpec((tn, w), lambda i, s: (i, 0))
    pair = lambda s: jnp.clip(s - N_CHUNKS, 0, N_PAIRS - 1)
    return pl.pallas_call(
        _peer_expert_kernel, grid=(n // tn, N_CHUNKS + N_PAIRS),
        out_shape=jax.ShapeDtypeStruct((n, D_MODEL), F32),
        in_specs=[row(D_MODEL), row(D_MODEL), row(hk), row(hk),
                  pl.BlockSpec((D_MODEL, E_CHUNK), lambda i, s: (0, jnp.minimum(s, N_CHUNKS - 1))),
                  pl.BlockSpec((E_CHUNK, D_MODEL), lambda i, s: (pair(s), 0)),
                  pl.BlockSpec((E_CHUNK, D_MODEL), lambda i, s: (pair(s) + N_PAIRS, 0))],
        out_specs=row(D_MODEL),
        scratch_shapes=[pltpu.VMEM((tn, hk), F32), pltpu.VMEM((tn * HALF_KEYS, N_KEYS), U32)],
        compiler_params=_params(("parallel", "arbitrary"), 56 * 1024 * 1024), name="peer_experts",
    )(x2, h, eidx, gate, ut, v, v)


def _final_norm_kernel(h_ref, g_ref, o_ref):
    o_ref[...] = _rms(h_ref[...], g_ref[...])


def final_norm(h, g, tm):
    n = h.shape[0]
    row = pl.BlockSpec((tm, D_MODEL), lambda i: (i, 0))
    return pl.pallas_call(
        _final_norm_kernel, grid=(n // tm,), out_shape=jax.ShapeDtypeStruct(h.shape, F32),
        in_specs=[row, pl.BlockSpec(g.shape, lambda i: (0, 0))], out_specs=row,
        compiler_params=_params(("parallel",)), name="final_norm",
    )(h, g)


def _block_diag(w):
    g, a, b = w.shape
    eye = jnp.eye(g, dtype=w.dtype)
    return (eye[:, None, :, None] * w[:, :, None, :]).reshape(g * a, g * b)


def _prep_layer(W, l):
    w_in = W["w_in"][l]
    return dict(
        g1=W["norm1_g"][l][None, :],
        wa=w_in[:, OFF_AQ:OFF_AF].astype(BF16),
        wb=w_in[:, OFF_BX:OFF_CQKV].astype(BF16),
        wc=jnp.concatenate([w_in[:, OFF_CQKV:OFF_CA], w_in[:, OFF_CG:D_IN]], axis=1).astype(BF16),
        ws=jnp.concatenate([w_in[:, OFF_AF:OFF_BX], w_in[:, OFF_CA:OFF_CG]], axis=1).T,
        bf=W["b_f"][l][:, None],
        lru_cw=W["lru_conv_w"][l], lru_cb=W["lru_conv_b"][l][None, :],
        lru_wa=_block_diag(W["lru_wa"][l]), lru_ba=W["lru_ba"][l][None, :],
        lru_wx=_block_diag(W["lru_wx"][l]), lru_bx=W["lru_bx"][l][None, :],
        lru_lam=W["lru_lambda"][l][None, :],
        gdn_cw=W["gdn_conv_w"][l], gdn_alog=W["gdn_A_log"][l][:, None], gdn_dt=W["gdn_dt_bias"][l][:, None],
        gdn_ng=jnp.tile(W["gdn_norm_g"][l], C_HEADS)[None, :],
        wo=W["w_out"][l].astype(BF16), g2=W["norm2_g"][l][None, :], wq=W["peer_wq"][l].astype(BF16),
        keys=W["peer_keys"][l], ut=W["peer_u"][l].T.astype(BF16), v=W["peer_v"][l].astype(BF16),
    )


def _pair_state(s):
    nb = s.shape[0]
    s = s.reshape(nb, C_HEADS // 2, 2, C_HEAD_DIM, C_HEAD_DIM)
    eye = jnp.eye(2, dtype=s.dtype)
    return (s[:, :, :, :, None, :] * eye[None, None, :, None, :, None]).reshape(nb, C_HEADS // 2, PAIR, PAIR)


def _unpair_state(s):
    nb = s.shape[0]
    s = s.reshape(nb, C_HEADS // 2, 2, C_HEAD_DIM, 2, C_HEAD_DIM)
    return jnp.stack([s[:, :, 0, :, 0, :], s[:, :, 1, :, 1, :]], axis=2).reshape(nb, C_HEADS, C_HEAD_DIM, C_HEAD_DIM)


def _trunk(x, nb, t_real, paged, lru_h, lru_buf, gdn_s, gdn_buf, layers, final_g, cfg):
    t_rows = x.shape[1]
    n = nb * t_rows
    h = x.reshape(n, D_MODEL)
    ones_hd = _block_diag(jnp.full((C_HEADS, C_HEAD_DIM, C_HEAD_DIM), 1.0 / C_HEAD_DIM, F32))
    outs = []
    for l, lw in enumerate(layers):
        q, k, v, kb, vb, zb, zc, st = in_proj(h, lw["g1"], lw["wa"], lw["wb"], lw["wc"], lw["ws"], cfg["tm"])
        if paged is None:
            lf, c = logf_rows(st, lw["bf"], nb, True)
            ya = attn_prompt(q, kb, vb, c.reshape(A_HEADS // 2, 2, n), nb, cfg["tq"])
            n_chunks = cfg["tc_gdn"] // GDN_CHUNK
            scal = st.reshape(N_SMALL, n // GDN_CHUNK, GDN_CHUNK).transpose(1, 0, 2)
        else:
            ck, cv, clt, pt = paged
            lf, _ = logf_rows(st, lw["bf"], 1, False)
            lfn = lf.reshape(A_HEADS, nb, t_rows).transpose(1, 0, 2)
            lfn = jnp.where(jnp.arange(t_rows) < t_real, lfn, 0.0)
            lfn = jnp.pad(lfn, ((0, 0), (0, 0), (0, PAGE_SIZE - t_rows)))
            ya = attn_sample(pt, q, ck, cv, clt, k, v, lfn, l, t_real)
            n_chunks = 1
            scal = jnp.pad(st.reshape(N_SMALL, nb, t_rows).transpose(1, 0, 2),
                           ((0, 0), (0, 0), (0, GDN_CHUNK - t_rows)))
        t_valid = min(t_real, cfg["tc_lru"])
        yb, lru_ht, lru_nb = lru_mixer(zb, lru_buf[l], lru_h[l][:, None, :], lw["lru_cw"], lw["lru_cb"],
                                       lw["lru_wa"], lw["lru_ba"], lw["lru_wx"], lw["lru_bx"], lw["lru_lam"],
                                       nb, cfg["tc_lru"], t_valid)
        t_valid = min(t_real, cfg["tc_gdn"])
        yc, gdn_st, gdn_nb = gdn_mixer(zc, scal, gdn_buf[l], _pair_state(gdn_s[l]), lw["gdn_cw"], lw["gdn_alog"],
                                       lw["gdn_dt"], lw["gdn_ng"], ones_hd, nb, cfg["tc_gdn"], t_valid, n_chunks)
        hmid, x2, qp = out_proj(h, ya, yb, yc, lw["wo"], lw["g2"], lw["wq"], cfg["tm"])
        eidx, gate = peer_topk(qp, lw["keys"], cfg["tn_topk"])
        h = peer_experts(x2, hmid, eidx, gate, lw["ut"], lw["v"], cfg["tn_peer"])
        outs.append((k.reshape(nb, t_rows, A_HEADS, A_HEAD_DIM)[:, :t_real],
                     v.reshape(nb, t_rows, A_HEADS, A_HEAD_DIM)[:, :t_real],
                     lf.reshape(A_HEADS, nb, t_rows).transpose(1, 2, 0)[:, :t_real],
                     lru_ht[:, 0, :], lru_nb, _unpair_state(gdn_st), gdn_nb))
    y = final_norm(h, final_g[None, :], cfg["tm"]).reshape(nb, t_rows, D_MODEL)[:, :t_real]
    return y, tuple(jnp.stack([o[i] for o in outs], axis=0) for i in range(len(outs[0])))


PROMPT_CFG = dict(tm=512, tq=512, tc_lru=512, tc_gdn=256, tn_topk=128, tn_peer=512)
SAMPLE_CFG = dict(tm=256, tq=None, tc_lru=SAMPLE_ROWS, tc_gdn=SAMPLE_ROWS, tn_topk=128, tn_peer=256)


def kernel(x_prompt, x_sample, cache_k, cache_v, cache_logf, page_table, state_lru_h, state_lru_conv,
           state_gdn_S, state_gdn_conv, norm1_g, w_in, b_f, lru_conv_w, lru_conv_b, lru_wa, lru_ba,
           lru_wx, lru_bx, lru_lambda, gdn_conv_w, gdn_A_log, gdn_dt_bias, gdn_norm_g, w_out, norm2_g,
           peer_wq, peer_keys, peer_u, peer_v, final_norm_g):
    W = dict(norm1_g=norm1_g, w_in=w_in, b_f=b_f, lru_conv_w=lru_conv_w, lru_conv_b=lru_conv_b, lru_wa=lru_wa,
             lru_ba=lru_ba, lru_wx=lru_wx, lru_bx=lru_bx, lru_lambda=lru_lambda, gdn_conv_w=gdn_conv_w,
             gdn_A_log=gdn_A_log, gdn_dt_bias=gdn_dt_bias, gdn_norm_g=gdn_norm_g, w_out=w_out, norm2_g=norm2_g,
             peer_wq=peer_wq, peer_keys=peer_keys, peer_u=peer_u, peer_v=peer_v)
    depth = w_in.shape[0]
    layers = [_prep_layer(W, l) for l in range(depth)]
    bp, tp = x_prompt.shape[:2]
    bs, ts = x_sample.shape[:2]

    ck = cache_k.transpose(0, 1, 3, 4, 2)
    cv = cache_v.transpose(0, 1, 3, 4, 2)
    clt = cache_logf.transpose(0, 1, 3, 2)
    xs = jnp.pad(x_sample, ((0, 0), (0, SAMPLE_ROWS - ts), (0, 0)))
    ys, ss = _trunk(xs, bs, ts, (ck, cv, clt, page_table), state_lru_h, state_lru_conv, state_gdn_S,
                    state_gdn_conv, layers, final_norm_g, SAMPLE_CFG)

    zeros = lambda *s: jnp.zeros((depth, bp) + s, F32)
    yp, sp = _trunk(x_prompt, bp, tp, None, zeros(B_WIDTH), zeros(CONV_W - 1, B_WIDTH),
                    zeros(C_HEADS, C_HEAD_DIM, C_HEAD_DIM), zeros(CONV_W - 1, 3 * C_WIDTH),
                    layers, final_norm_g, PROMPT_CFG)

    return (yp, ys, sp[0], sp[1], sp[2], ss[0], ss[1], ss[2], sp[3], ss[3], sp[4], ss[4], sp[5], ss[5], sp[6], ss[6])
```

```python
import functools

import jax
import jax.numpy as jnp
from jax import lax
from jax.experimental import pallas as pl
from jax.experimental.pallas import tpu as pltpu

F32 = jnp.float32
BF16 = jnp.bfloat16
I32 = jnp.int32
HI = lax.Precision.HIGHEST

D_MODEL = 1024
A_HEADS = 8
A_HEAD_DIM = 64
A_WIDTH = A_HEADS * A_HEAD_DIM
B_WIDTH = 256
B_BLOCKS = 4
LRU_C = 8.0
C_HEADS = 4
C_HEAD_DIM = 64
C_WIDTH = C_HEADS * C_HEAD_DIM
GDN_CHUNK = 64
CONV_W = 4
P_HEADS = 8
P_DK = 128
N_KEYS = 128
N_EXPERTS = N_KEYS * N_KEYS
P_TOPK = 16
PAGE_SIZE = 128
EPS = 1e-6
SAMPLE_ROWS = 8
N_SMALL = 16

OFF_AQ = 0
OFF_AF = OFF_AQ + 3 * A_WIDTH
OFF_BX = OFF_AF + A_HEADS
OFF_CQKV = OFF_BX + 2 * B_WIDTH
OFF_CA = OFF_CQKV + 3 * C_WIDTH
OFF_CG = OFF_CA + 2 * C_HEADS
D_IN = OFF_CG + C_WIDTH

VMEM_LIMIT = 48 * 1024 * 1024


def _params(sem, vmem=VMEM_LIMIT):
    return pltpu.CompilerParams(dimension_semantics=sem, vmem_limit_bytes=vmem)


def _dot(a, b, precision=None):
    return jnp.dot(a, b, precision=precision, preferred_element_type=F32)


def _dot_nt(a, b, precision=None):
    return lax.dot_general(a, b, (((1,), (1,)), ((), ())), precision=precision, preferred_element_type=F32)


def _dot_tn(a, b, precision=None):
    return lax.dot_general(a, b, (((0,), (0,)), ((), ())), precision=precision, preferred_element_type=F32)


_DOTS = {"nn": _dot, "nt": _dot_nt, "tn": _dot_tn}


def _split_bf16(a):
    hi = a.astype(BF16)
    return hi, (a - hi.astype(F32)).astype(BF16)


def _mm(a, b, prec, kind="nn"):
    dot = _DOTS[kind]
    if prec == "bf":
        return dot(a.astype(BF16), b.astype(BF16))
    if prec == "x2l":
        ah, al = _split_bf16(a)
        bb = b.astype(BF16)
        return dot(ah, bb) + dot(al, bb)
    if prec == "x3":
        ah, al = _split_bf16(a)
        bh, bl = _split_bf16(b)
        return dot(ah, bh) + (dot(ah, bl) + dot(al, bh))
    return dot(a, b, HI)


def _rms(x, g):
    return x * lax.rsqrt(jnp.mean(x * x, axis=-1, keepdims=True) + EPS) * g


def _gelu_erf(x):
    return 0.5 * x * (1.0 + lax.erf(x * 0.7071067811865476))


def _iota(shape, dim):
    return lax.broadcasted_iota(I32, shape, dim)


def _inproj_kernel(x_ref, g_ref, wa_ref, wb_ref, wc_ref, ws_ref,
                   q_ref, k_ref, v_ref, kb_ref, vb_ref, zb_ref, zc_ref, st_ref):
    hn = _rms(x_ref[...], g_ref[...])
    hb = hn.astype(BF16)
    za = _dot(hb, wa_ref[...])
    q_ref[...] = (za[:, :A_WIDTH] * (A_HEAD_DIM ** -0.5)).astype(BF16)
    k = za[:, A_WIDTH:2 * A_WIDTH]
    v = za[:, 2 * A_WIDTH:]
    k_ref[...] = k
    v_ref[...] = v
    kb_ref[...] = k.astype(BF16)
    vb_ref[...] = v.astype(BF16)
    zb_ref[...] = _dot(hb, wb_ref[...])
    zc_ref[...] = _dot(hb, wc_ref[...])
    st_ref[...] = _dot_nt(ws_ref[...], hn, HI)


def in_proj(h, g, wa, wb, wc, ws, tm):
    n = h.shape[0]
    grid = (n // tm,)
    row = lambda w: pl.BlockSpec((tm, w), lambda i: (i, 0))
    full = lambda a: pl.BlockSpec(a.shape, lambda i: (0,) * a.ndim)
    out_shape = [
        jax.ShapeDtypeStruct((n, A_WIDTH), BF16),
        jax.ShapeDtypeStruct((n, A_WIDTH), F32),
        jax.ShapeDtypeStruct((n, A_WIDTH), F32),
        jax.ShapeDtypeStruct((n, A_WIDTH), BF16),
        jax.ShapeDtypeStruct((n, A_WIDTH), BF16),
        jax.ShapeDtypeStruct((n, 2 * B_WIDTH), F32),
        jax.ShapeDtypeStruct((n, 4 * C_WIDTH), F32),
        jax.ShapeDtypeStruct((N_SMALL, n), F32),
    ]
    out_specs = [row(A_WIDTH)] * 5 + [row(2 * B_WIDTH), row(4 * C_WIDTH),
                                      pl.BlockSpec((N_SMALL, tm), lambda i: (0, i))]
    return pl.pallas_call(
        _inproj_kernel, grid=grid, out_shape=out_shape,
        in_specs=[row(D_MODEL), full(g), full(wa), full(wb), full(wc), full(ws)],
        out_specs=out_specs, compiler_params=_params(("parallel",)), name="in_proj",
    )(h, g, wa, wb, wc, ws)


CUMSUM_CHUNK = 512


def _logf_kernel(st_ref, bf_ref, lf_ref, c_ref, *, cumsum):
    lf = jax.nn.log_sigmoid(st_ref[0:A_HEADS, :] + bf_ref[...])
    lf_ref[...] = lf
    if not cumsum:
        c_ref[...] = lf
        return
    t = lf.shape[1]
    ch = min(CUMSUM_CHUNK, t)
    tri = (_iota((ch, ch), 0) <= _iota((ch, ch), 1)).astype(F32)
    carry = jnp.zeros((A_HEADS, 1), F32)
    for j in range(t // ch):
        cs = _dot(lf[:, j * ch:(j + 1) * ch], tri, HI) + carry
        c_ref[:, j * ch:(j + 1) * ch] = cs
        carry = cs[:, ch - 1:ch]


def logf_rows(st, bf, nb, cumsum):
    n = st.shape[1]
    t = n // nb
    spec_in = pl.BlockSpec((N_SMALL, t), lambda b: (0, b))
    spec_out = pl.BlockSpec((A_HEADS, t), lambda b: (0, b))
    return pl.pallas_call(
        functools.partial(_logf_kernel, cumsum=cumsum), grid=(nb,),
        out_shape=[jax.ShapeDtypeStruct((A_HEADS, n), F32)] * 2,
        in_specs=[spec_in, pl.BlockSpec(bf.shape, lambda b: (0, 0))],
        out_specs=[spec_out, spec_out], compiler_params=_params(("parallel",)), name="logf",
    )(st, bf)


def _attn_prompt_kernel(qi_ref, ki_ref, q_ref, k_ref, v_ref, c_ref, o_ref, m_ref, l_ref, acc_ref):
    step = pl.program_id(2)
    qi = qi_ref[step]
    ki = ki_ref[step]
    tq, tk = q_ref.shape[0], k_ref.shape[0]
    w = 2 * A_HEAD_DIM
    lane = _iota((1, w), 1)

    @pl.when(ki == 0)
    def _():
        m_ref[...] = jnp.full(m_ref.shape, -jnp.inf, F32)
        l_ref[...] = jnp.zeros(l_ref.shape, F32)
        acc_ref[...] = jnp.zeros(acc_ref.shape, F32)

    def block(masked):
        q = q_ref[...]
        k = k_ref[...]
        v = v_ref[...]
        for h in range(2):
            head = (lane >= A_HEAD_DIM) if h else (lane < A_HEAD_DIM)
            s = _dot_nt(jnp.where(head, q, jnp.zeros_like(q)), k) - c_ref[0, h:h + 1, :]
            if masked:
                s = jnp.where(_iota((tq, tk), 1) <= _iota((tq, tk), 0), s, -jnp.inf)
            m_prev = m_ref[h]
            m_new = jnp.maximum(m_prev, jnp.max(s, axis=1, keepdims=True))
            alpha = jnp.exp(m_prev - m_new)
            p = jnp.exp(s - jnp.concatenate([m_new] * (tk // w), axis=1))
            l_ref[h] = alpha * l_ref[h] + jnp.sum(p, axis=1, keepdims=True)
            acc_ref[h] = alpha * acc_ref[h] + _dot(p.astype(BF16), v)
            m_ref[h] = m_new

    @pl.when(ki < qi)
    def _():
        block(False)

    @pl.when(ki == qi)
    def _():
        block(True)
        o0 = acc_ref[0] / l_ref[0]
        o1 = acc_ref[1] / l_ref[1]
        o_ref[...] = jnp.where(lane < A_HEAD_DIM, o0, o1).astype(o_ref.dtype)


def attn_prompt(q, kb, vb, c, nb, tq):
    n = q.shape[0]
    t = n // nb
    nq = t // tq
    w = 2 * A_HEAD_DIM
    pairs = [(i, j) for i in range(nq) for j in range(i + 1)]
    qi_list = jnp.asarray([i for i, _ in pairs], I32)
    ki_list = jnp.asarray([j for _, j in pairs], I32)
    qspec = pl.BlockSpec((tq, w), lambda b, p, s, qi, ki: (b * nq + qi[s], p))
    kspec = pl.BlockSpec((tq, w), lambda b, p, s, qi, ki: (b * nq + ki[s], p))
    cspec = pl.BlockSpec((1, 2, tq), lambda b, p, s, qi, ki: (p, 0, b * nq + ki[s]))
    grid_spec = pltpu.PrefetchScalarGridSpec(
        num_scalar_prefetch=2, grid=(nb, A_HEADS // 2, len(pairs)),
        in_specs=[qspec, kspec, kspec, cspec], out_specs=qspec,
        scratch_shapes=[pltpu.VMEM((2, tq, w), F32), pltpu.VMEM((2, tq, w), F32), pltpu.VMEM((2, tq, w), F32)])
    return pl.pallas_call(
        _attn_prompt_kernel, grid_spec=grid_spec,
        out_shape=jax.ShapeDtypeStruct((n, A_WIDTH), BF16),
        compiler_params=_params(("parallel", "parallel", "arbitrary")), name="attn_prompt",
    )(qi_list, ki_list, q, kb, vb, c)


PAGE_GROUP = 8


def _attn_sample_kernel(pt_ref, q_ref, kn_ref, vn_ref, ln_ref, ck_hbm, cv_hbm, cl_hbm, o_ref,
                        kbuf, vbuf, lbuf, c_ref, qh_ref, m_ref, l_ref, acc_ref, ksem, vsem, lsem,
                        *, layer, n_pages, t_new):
    b = pl.program_id(0)
    nr = A_HEADS * SAMPLE_ROWS
    hd = A_HEAD_DIM
    n_groups = n_pages // PAGE_GROUP

    def kv_copies(g, slot):
        cps = []
        for i in range(PAGE_GROUP):
            pg = pt_ref[b, g * PAGE_GROUP + i]
            cps.append(pltpu.make_async_copy(ck_hbm.at[layer, pg], kbuf.at[slot, i], ksem.at[slot]))
            cps.append(pltpu.make_async_copy(cv_hbm.at[layer, pg], vbuf.at[slot, i], vsem.at[slot]))
        return cps

    def lf_copy(j):
        return pltpu.make_async_copy(cl_hbm.at[layer, pt_ref[b, j]], lbuf.at[j], lsem)

    for j in range(n_pages):
        lf_copy(j).start()
    for cp in kv_copies(0, 0):
        cp.start()

    q = q_ref[...].astype(F32)
    for h in range(A_HEADS):
        qh_ref[h] = q[:, h * hd:(h + 1) * hd]
    m_ref[...] = jnp.full(m_ref.shape, -jnp.inf, F32)
    l_ref[...] = jnp.zeros(l_ref.shape, F32)
    acc_ref[...] = jnp.zeros(acc_ref.shape, F32)

    for j in range(n_pages):
        lf_copy(j).wait()
    tri = (_iota((PAGE_SIZE, PAGE_SIZE), 0) <= _iota((PAGE_SIZE, PAGE_SIZE), 1)).astype(F32)
    nrow = n_pages * A_HEADS
    cs = _dot(lbuf[...].reshape(nrow, PAGE_SIZE), tri, HI)
    totals = jnp.broadcast_to(cs[:, PAGE_SIZE - 1:PAGE_SIZE], (nrow, PAGE_SIZE))
    r, cc = _iota((nrow, nrow), 0), _iota((nrow, nrow), 1)
    earlier = ((cc // A_HEADS < r // A_HEADS) & (cc % A_HEADS == r % A_HEADS)).astype(F32)
    c_all = cs + _dot(earlier, totals, HI)
    c_ref[...] = c_all.reshape(n_pages, A_HEADS, PAGE_SIZE)
    carry = c_all[nrow - A_HEADS:, PAGE_SIZE - 1:PAGE_SIZE]

    def update(s, weighted):
        m_prev = m_ref[...]
        m_new = jnp.maximum(m_prev, jnp.max(s, axis=1, keepdims=True))
        alpha = jnp.exp(m_prev - m_new)
        p = jnp.exp(s - m_new)
        l_ref[...] = alpha * l_ref[...] + jnp.sum(p, axis=1, keepdims=True)
        pv = jnp.concatenate([weighted(h, p[h * SAMPLE_ROWS:(h + 1) * SAMPLE_ROWS, :].astype(BF16))
                              for h in range(A_HEADS)], axis=0)
        acc_ref[...] = alpha * acc_ref[...] + pv
        m_ref[...] = m_new

    def group(g, carry_):
        slot = g % 2

        @pl.when(g + 1 < n_groups)
        def _():
            for cp in kv_copies(g + 1, 1 - slot):
                cp.start()

        for cp in kv_copies(g, slot):
            cp.wait()
        kb = kbuf.at[slot]
        vb = vbuf.at[slot]
        tiles = []
        for i in range(PAGE_GROUP):
            c = c_ref[g * PAGE_GROUP + i]
            tiles.append(jnp.concatenate(
                [_dot(qh_ref[h].astype(BF16), kb[i, h].astype(BF16)) - c[h:h + 1, :] for h in range(A_HEADS)], axis=0))

        def weighted(h, ph):
            out = _dot_nt(ph[:, 0:PAGE_SIZE], vb[0, h].astype(BF16))
            for i in range(1, PAGE_GROUP):
                out = out + _dot_nt(ph[:, i * PAGE_SIZE:(i + 1) * PAGE_SIZE], vb[i, h].astype(BF16))
            return out

        update(jnp.concatenate(tiles, axis=1), weighted)
        return carry_

    lax.fori_loop(0, n_groups, group, 0)

    pad = jnp.zeros((PAGE_SIZE - SAMPLE_ROWS, hd), F32)
    rows = lambda ref, h: jnp.concatenate([ref[:, h * hd:(h + 1) * hd], pad], axis=0).astype(BF16)
    c_new = _dot(ln_ref[...], tri, HI) + carry
    visible = _iota((nr, PAGE_SIZE), 1) <= _iota((nr, PAGE_SIZE), 0) % SAMPLE_ROWS
    s_new = jnp.concatenate([_dot_nt(qh_ref[h].astype(BF16), rows(kn_ref, h)) - c_new[h:h + 1, :]
                             for h in range(A_HEADS)], axis=0)
    update(jnp.where(visible, s_new, -jnp.inf), lambda h, ph: _dot(ph, rows(vn_ref, h)))

    o = acc_ref[...] / l_ref[...]
    o = jnp.concatenate([o[h * SAMPLE_ROWS:(h + 1) * SAMPLE_ROWS, :] for h in range(A_HEADS)], axis=1)
    o_ref[...] = jnp.where(_iota(o.shape, 0) < t_new, o, 0.0).astype(o_ref.dtype)


def attn_sample(page_table, q, cache_k, cache_v, cache_lt, kn, vn, lfn, layer, t_new):
    nb, n_pages = page_table.shape
    assert n_pages % PAGE_GROUP == 0
    nr = A_HEADS * SAMPLE_ROWS
    seq = lambda w: pl.BlockSpec((SAMPLE_ROWS, w), lambda b, pt: (b, 0))
    hbm = pl.BlockSpec(memory_space=pl.ANY)
    page_buf = pltpu.VMEM((2, PAGE_GROUP, A_HEADS, A_HEAD_DIM, PAGE_SIZE), F32)
    grid_spec = pltpu.PrefetchScalarGridSpec(
        num_scalar_prefetch=1, grid=(nb,),
        in_specs=[seq(A_WIDTH), seq(A_WIDTH), seq(A_WIDTH),
                  pl.BlockSpec((None, A_HEADS, PAGE_SIZE), lambda b, pt: (b, 0, 0)), hbm, hbm, hbm],
        out_specs=seq(A_WIDTH),
        scratch_shapes=[page_buf, page_buf,
                        pltpu.VMEM((n_pages, A_HEADS, PAGE_SIZE), F32), pltpu.VMEM((n_pages, A_HEADS, PAGE_SIZE), F32),
                        pltpu.VMEM((A_HEADS, SAMPLE_ROWS, A_HEAD_DIM), F32), pltpu.VMEM((nr, 1), F32),
                        pltpu.VMEM((nr, 1), F32), pltpu.VMEM((nr, A_HEAD_DIM), F32),
                        pltpu.SemaphoreType.DMA((2,)), pltpu.SemaphoreType.DMA((2,)), pltpu.SemaphoreType.DMA(())])
    return pl.pallas_call(
        functools.partial(_attn_sample_kernel, layer=layer, n_pages=n_pages, t_new=t_new), grid_spec=grid_spec,
        out_shape=jax.ShapeDtypeStruct(q.shape, BF16),
        compiler_params=_params(("arbitrary",)), name="attn_sample",
    )(page_table, q, kn, vn, lfn, cache_k, cache_v, cache_lt)


CONV_PAD = 8


def _conv_block(xp_ref, x, w_ref, tc):
    xp_ref[CONV_PAD:CONV_PAD + tc, :] = x
    y = xp_ref[CONV_PAD - 3:CONV_PAD - 3 + tc, :] * w_ref[0:1, :]
    for i in range(1, CONV_W):
        y = y + xp_ref[CONV_PAD - 3 + i:CONV_PAD - 3 + i + tc, :] * w_ref[i:i + 1, :]
    return y


def _lru_kernel(z_ref, buf_ref, h0_ref, cw_ref, cb_ref, wa_ref, ba_ref, wx_ref, bx_ref, lam_ref,
                y_ref, ht_ref, nb_ref, xp_ref, hc_ref, *, t_valid):
    ti = pl.program_id(1)
    tc = z_ref.shape[0]

    @pl.when(ti == 0)
    def _():
        xp_ref[CONV_PAD - 3:CONV_PAD, :] = buf_ref[...]
        hc_ref[...] = h0_ref[...]

    x = z_ref[:, :B_WIDTH]
    xc = _conv_block(xp_ref, x, cw_ref, tc) + cb_ref[...]
    r = jax.nn.sigmoid(_dot(xc, wa_ref[...], HI) + ba_ref[...])
    ig = jax.nn.sigmoid(_dot(xc, wx_ref[...], HI) + bx_ref[...])
    log_a = -LRU_C * r * jax.nn.softplus(-lam_ref[...])
    a = jnp.exp(log_a)
    b = jnp.sqrt(1.0 - jnp.exp(2.0 * log_a)) * (ig * xc)
    row = _iota((tc, B_WIDTH), 0)
    d = 1
    while d < tc:
        keep = row >= d
        a_prev = jnp.where(keep, pltpu.roll(a, d, 0), 1.0)
        b_prev = jnp.where(keep, pltpu.roll(b, d, 0), 0.0)
        b = a * b_prev + b
        a = a * a_prev
        d *= 2
    hs = a * hc_ref[...] + b
    y_ref[...] = (hs * jax.nn.gelu(z_ref[:, B_WIDTH:])).astype(y_ref.dtype)
    hc_ref[...] = hs[t_valid - 1:t_valid, :]
    xp_ref[CONV_PAD - 3:CONV_PAD, :] = xp_ref[CONV_PAD + t_valid - 3:CONV_PAD + t_valid, :]

    @pl.when(ti == pl.num_programs(1) - 1)
    def _():
        ht_ref[...] = hc_ref[...]
        nb_ref[...] = xp_ref[CONV_PAD - 3:CONV_PAD, :]


def lru_mixer(zb, buf0, h0, cw, cb, wa, ba, wx, bx, lam, nb, tc, t_valid):
    n = zb.shape[0]
    nt = n // nb // tc
    full = lambda a: pl.BlockSpec(a.shape, lambda b, t: (0,) * a.ndim)
    per_b = lambda r: pl.BlockSpec((None, r, B_WIDTH), lambda b, t: (b, 0, 0))
    return pl.pallas_call(
        functools.partial(_lru_kernel, t_valid=t_valid), grid=(nb, nt),
        out_shape=[jax.ShapeDtypeStruct((n, B_WIDTH), BF16), jax.ShapeDtypeStruct((nb, 1, B_WIDTH), F32),
                   jax.ShapeDtypeStruct((nb, CONV_W - 1, B_WIDTH), F32)],
        in_specs=[pl.BlockSpec((tc, 2 * B_WIDTH), lambda b, t: (b * nt + t, 0)), per_b(CONV_W - 1), per_b(1),
                  full(cw), full(cb), full(wa), full(ba), full(wx), full(bx), full(lam)],
        out_specs=[pl.BlockSpec((tc, B_WIDTH), lambda b, t: (b * nt + t, 0)), per_b(1), per_b(CONV_W - 1)],
        scratch_shapes=[pltpu.VMEM((CONV_PAD + tc, B_WIDTH), F32), pltpu.VMEM((1, B_WIDTH), F32)],
        compiler_params=_params(("parallel", "arbitrary")), name="lru",
    )(zb, buf0, h0, cw, cb, wa, ba, wx, bx, lam)


PAIR = 2 * C_HEAD_DIM
GDN_INV_PREC = "bf"
GDN_PREC = "bf"
GDN_LOCKSTEP = 4


def _stack_heads(x, lo_mask):
    return jnp.concatenate([jnp.where(lo_mask, x, 0.0), jnp.where(lo_mask, 0.0, x)], axis=0)


def _row_to_col(row, eye):
    return jnp.sum(jnp.where(eye, row, 0.0), axis=1, keepdims=True)


def _gdn_kernel(z_ref, sc_ref, buf_ref, s0_ref, cw_ref, alog_ref, dt_ref, ng_ref, ones_hd_ref,
                y_ref, st_ref, nb_ref, xp_ref, s_ref, o_ref, ys_ref, wy_ref, eg_ref, *, t_valid, n_chunks):
    ti = pl.program_id(1)
    tc = z_ref.shape[0]
    c = GDN_CHUNK
    qkv_w = 3 * C_WIDTH

    @pl.when(ti == 0)
    def _():
        xp_ref[CONV_PAD - 3:CONV_PAD, :] = buf_ref[...]
        s_ref[...] = s0_ref[...]

    y = jax.nn.silu(_conv_block(xp_ref, z_ref[:, :qkv_w], cw_ref, tc))
    xp_ref[CONV_PAD - 3:CONV_PAD, :] = xp_ref[CONV_PAD + t_valid - 3:CONV_PAD + t_valid, :]
    padded = tc < n_chunks * c
    if padded:
        y = jnp.concatenate([y, jnp.zeros((n_chunks * c - tc, qkv_w), F32)], axis=0)
    ones_hd = ones_hd_ref[...]

    def l2n(x):
        return x * lax.rsqrt(_mm(x * x, ones_hd, "x2l") * C_HEAD_DIM + EPS)

    ys_ref[:, :C_WIDTH] = l2n(y[:, :C_WIDTH]) * (C_HEAD_DIM ** -0.5)
    ys_ref[:, C_WIDTH:2 * C_WIDTH] = l2n(y[:, C_WIDTH:2 * C_WIDTH])
    ys_ref[:, 2 * C_WIDTH:] = y[:, 2 * C_WIDTH:]

    r128 = _iota((PAIR, PAIR), 0)
    c128 = _iota((PAIR, PAIR), 1)
    same = (r128 // c) == (c128 // c)
    eye = r128 == c128
    incl = same & (c128 <= r128)
    strict = same & (c128 < r128)
    tri_bd = (same & (r128 <= c128)).astype(F32)
    last_sel = same & (c128 % c == c - 1)
    lo_lane = _iota((1, PAIR), 1) < c
    lane_t = _iota((1, PAIR), 1) % c
    ident = eye.astype(F32)

    def prepare(ci, p):
        sc = sc_ref[ci]
        g4 = -jnp.exp(alog_ref[...]) * jax.nn.softplus(sc[8:12, :] + dt_ref[...])
        b4 = jax.nn.sigmoid(sc[12:16, :])
        g_row = jnp.concatenate([g4[2 * p:2 * p + 1, :], g4[2 * p + 1:2 * p + 2, :]], axis=1)
        b_row = jnp.concatenate([b4[2 * p:2 * p + 1, :], b4[2 * p + 1:2 * p + 2, :]], axis=1)
        if padded:
            ok = lane_t < t_valid
            g_row = jnp.where(ok, g_row, 0.0)
            b_row = jnp.where(ok, b_row, 0.0)
        gc_row = _dot(jnp.broadcast_to(g_row, (8, PAIR)), tri_bd, HI)[0:1, :]
        gc_col = _row_to_col(gc_row, eye)
        b_col = _row_to_col(b_row, eye)
        gl_col = jnp.sum(jnp.where(last_sel, gc_row, 0.0), axis=1, keepdims=True)
        decay = jnp.where(incl, jnp.exp(jnp.where(incl, gc_col - gc_row, 0.0)), 0.0)
        rows = slice(ci * c, (ci + 1) * c)
        q_st = _stack_heads(ys_ref[rows, p * PAIR:(p + 1) * PAIR], lo_lane)
        k_st = _stack_heads(ys_ref[rows, C_WIDTH + p * PAIR:C_WIDTH + (p + 1) * PAIR], lo_lane)
        v_st = _stack_heads(ys_ref[rows, 2 * C_WIDTH + p * PAIR:2 * C_WIDTH + (p + 1) * PAIR], lo_lane)
        kb_st = k_st * b_col
        wy_ref[3, ci, p] = q_st * jnp.exp(gc_col)
        wy_ref[4, ci, p] = k_st * jnp.exp(gl_col - gc_col)
        eg_ref[ci, p] = jnp.exp(gl_col)
        return dict(decay=decay, q=q_st, k=k_st, kb=kb_st, vb=v_st * b_col, kbe=kb_st * jnp.exp(gc_col))

    chains = [(ci, p) for ci in range(n_chunks) for p in range(C_HEADS // 2)]
    for g0 in range(0, len(chains), GDN_LOCKSTEP):
        group = chains[g0:g0 + GDN_LOCKSTEP]
        st = [prepare(ci, p) for ci, p in group]
        lmat = [jnp.where(strict, _mm(d["kb"], d["k"], GDN_PREC, "nt") * d["decay"], 0.0) for d in st]
        tinv = [ident - m for m in lmat]
        pw = lmat
        for _ in range(5):
            pw = [_mm(x, x, GDN_INV_PREC) for x in pw]
            tinv = [t + _mm(t, x, GDN_INV_PREC) for t, x in zip(tinv, pw)]
        for (ci, p), d, t in zip(group, st, tinv):
            wy_ref[0, ci, p] = _mm(t, d["vb"], GDN_INV_PREC)
            wy_ref[1, ci, p] = _mm(t, d["kbe"], GDN_INV_PREC)
            wy_ref[2, ci, p] = _mm(d["q"], d["k"], GDN_PREC, "nt") * d["decay"]

    for ci in range(n_chunks):
        for p in range(C_HEADS // 2):
            s = s_ref[p]
            v_new = wy_ref[0, ci, p] - _mm(wy_ref[1, ci, p], s, GDN_PREC)
            o_st = _mm(wy_ref[3, ci, p], s, GDN_PREC) + _mm(wy_ref[2, ci, p], v_new, GDN_PREC)
            o_ref[ci * c:(ci + 1) * c, p * PAIR:(p + 1) * PAIR] = o_st[:c, :] + o_st[c:, :]
            s_ref[p] = s * eg_ref[ci, p] + _mm(wy_ref[4, ci, p], v_new, GDN_PREC, "tn")

    o = o_ref[0:tc, :]
    ms = _mm(o * o, ones_hd, "x2l")
    on = o * lax.rsqrt(ms + EPS) * ng_ref[...]
    y_ref[...] = (on * jax.nn.silu(z_ref[:, qkv_w:])).astype(y_ref.dtype)

    @pl.when(ti == pl.num_programs(1) - 1)
    def _():
        st_ref[...] = s_ref[...]
        nb_ref[...] = xp_ref[CONV_PAD - 3:CONV_PAD, :]


def gdn_mixer(zc, scal, buf0, s0, cw, alog, dt, ng, ones_hd, nb, tc, t_valid, n_chunks):
    n = zc.shape[0]
    nt = n // nb // tc
    qkv_w = 3 * C_WIDTH
    full = lambda a: pl.BlockSpec(a.shape, lambda b, t: (0,) * a.ndim)
    return pl.pallas_call(
        functools.partial(_gdn_kernel, t_valid=t_valid, n_chunks=n_chunks), grid=(nb, nt),
        out_shape=[jax.ShapeDtypeStruct((n, C_WIDTH), BF16),
                   jax.ShapeDtypeStruct((nb, C_HEADS // 2, PAIR, PAIR), F32),
                   jax.ShapeDtypeStruct((nb, CONV_W - 1, qkv_w), F32)],
        in_specs=[pl.BlockSpec((tc, 4 * C_WIDTH), lambda b, t: (b * nt + t, 0)),
                  pl.BlockSpec((n_chunks, N_SMALL, GDN_CHUNK), lambda b, t: (b * nt + t, 0, 0)),
                  pl.BlockSpec((None, CONV_W - 1, qkv_w), lambda b, t: (b, 0, 0)),
                  pl.BlockSpec((None, C_HEADS // 2, PAIR, PAIR), lambda b, t: (b, 0, 0, 0)),
                  full(cw), full(alog), full(dt), full(ng), full(ones_hd)],
        out_specs=[pl.BlockSpec((tc, C_WIDTH), lambda b, t: (b * nt + t, 0)),
                   pl.BlockSpec((None, C_HEADS // 2, PAIR, PAIR), lambda b, t: (b, 0, 0, 0)),
                   pl.BlockSpec((None, CONV_W - 1, qkv_w), lambda b, t: (b, 0, 0))],
        scratch_shapes=[pltpu.VMEM((CONV_PAD + tc, qkv_w), F32),
                        pltpu.VMEM((C_HEADS // 2, PAIR, PAIR), F32),
                        pltpu.VMEM((n_chunks * GDN_CHUNK, C_WIDTH), F32),
                        pltpu.VMEM((n_chunks * GDN_CHUNK, qkv_w), F32),
                        pltpu.VMEM((5, n_chunks, C_HEADS // 2, PAIR, PAIR), F32),
                        pltpu.VMEM((n_chunks, C_HEADS // 2, PAIR, 1), F32)],
        compiler_params=_params(("parallel", "arbitrary")), name="gdn",
    )(zc, scal, buf0, s0, cw, alog, dt, ng, ones_hd)


def _outproj_kernel(h_ref, ya_ref, yb_ref, yc_ref, wo_ref, g_ref, wq_ref, hn_ref, x2_ref, q_ref):
    mix = jnp.concatenate([ya_ref[...], yb_ref[...], yc_ref[...]], axis=1)
    h = h_ref[...] + _dot(mix, wo_ref[...])
    hn_ref[...] = h
    x2 = _rms(h, g_ref[...]).astype(BF16)
    x2_ref[...] = x2
    q = _dot(x2, wq_ref[...])
    for hp in range(N_SUBQ):
        q_ref[hp] = q[:, hp * P_DK:(hp + 1) * P_DK]


N_SUBQ = 2 * P_HEADS


def out_proj(h, ya, yb, yc, wo, g, wq, tm):
    n = h.shape[0]
    row = lambda w: pl.BlockSpec((tm, w), lambda i: (i, 0))
    full = lambda a: pl.BlockSpec(a.shape, lambda i: (0,) * a.ndim)
    return pl.pallas_call(
        _outproj_kernel, grid=(n // tm,),
        out_shape=[jax.ShapeDtypeStruct((n, D_MODEL), F32), jax.ShapeDtypeStruct((n, D_MODEL), BF16),
                   jax.ShapeDtypeStruct((N_SUBQ, n, P_DK), F32)],
        in_specs=[row(D_MODEL), row(A_WIDTH), row(B_WIDTH), row(C_WIDTH), full(wo), full(g), full(wq)],
        out_specs=[row(D_MODEL), row(D_MODEL), pl.BlockSpec((N_SUBQ, tm, P_DK), lambda i: (0, i, 0))],
        compiler_params=_params(("parallel",)), name="out_proj",
    )(h, ya, yb, yc, wo, g, wq)


def _top16_rows(s, payload=None):
    nrow = s.shape[0]
    row = _iota(s.shape, 0)
    vals, pays = [], []
    for _ in range(P_TOPK):
        m = jnp.max(s, axis=0, keepdims=True)
        pos = jnp.min(jnp.where(s == m, row, nrow), axis=0, keepdims=True)
        hit = row == pos
        vals.append(m)
        pays.append(pos if payload is None else jnp.max(jnp.where(hit, payload, -1), axis=0, keepdims=True))
        s = jnp.where(hit, -jnp.inf, s)
    return jnp.concatenate(vals, axis=0), jnp.concatenate(pays, axis=0)


def _head_retrieval(keys0, keys1, q0, q1):
    s0, i0 = _top16_rows(_dot_nt(keys0, q0, HI))
    s1, i1 = _top16_rows(_dot_nt(keys1, q1, HI))
    return _pair_top16(s0, i0, s1, i1)


def _pair_top16(s0, i0, s1, i1):
    tn = s0.shape[1]
    cand = [s0[0:1, :] + s1]
    cidx = [i0[0:1, :] * N_KEYS + i1]
    b8 = _iota((8, tn), 0)
    for a in range(1, 8):
        va = s0[a:a + 1, :] + s1[0:8, :]
        nb = P_TOPK // (a + 1)
        cand.append(va if nb >= 8 else jnp.where(b8 < nb, va, -jnp.inf))
        cidx.append(i0[a:a + 1, :] * N_KEYS + i1[0:8, :])
    cand.append(s0[8:16, :] + s1[0:1, :])
    cidx.append(i0[8:16, :] * N_KEYS + i1[0:1, :])
    best, e = _top16_rows(jnp.concatenate(cand, axis=0), jnp.concatenate(cidx, axis=0))
    w = jnp.exp(best - best[0:1, :])
    return e, w / jnp.sum(w, axis=0, keepdims=True)


def _peer_topk_kernel(q_ref, keys_ref, eidx_ref, gate_ref):
    eidx, gates = [], []
    for h in range(P_HEADS):
        e, g = _head_retrieval(keys_ref[h, 0], keys_ref[h, 1], q_ref[2 * h], q_ref[2 * h + 1])
        eidx.append(e)
        gates.append(g)
    eidx_ref[...] = jnp.concatenate(eidx, axis=0).T
    gate_ref[...] = jnp.concatenate(gates, axis=0).T


def peer_topk(q, keys, tn):
    n = q.shape[1]
    hk = P_HEADS * P_TOPK
    return pl.pallas_call(
        _peer_topk_kernel, grid=(n // tn,),
        out_shape=[jax.ShapeDtypeStruct((n, hk), I32), jax.ShapeDtypeStruct((n, hk), F32)],
        in_specs=[pl.BlockSpec((N_SUBQ, tn, P_DK), lambda i: (0, i, 0)), pl.BlockSpec(keys.shape, lambda i: (0, 0, 0, 0))],
        out_specs=[pl.BlockSpec((tn, hk), lambda i: (i, 0))] * 2,
        compiler_params=_params(("parallel",)), name="peer_topk",
    )(q, keys)


E_CHUNK = 1024
SLABS = E_CHUNK // N_KEYS
KEY_BITS = N_KEYS.bit_length() - 1


N_CHUNKS = N_EXPERTS // E_CHUNK
N_PAIRS = N_CHUNKS // 2
HALF_KEYS = N_KEYS // 2
TOKEN_UNROLL = 8
U32 = jnp.uint32
HI_HALF = 0xFFFF0000


def _expert_scores(step, x_ref, ut_ref, eidx_ref, pf_ref):
    e = eidx_ref[...]
    ii = jnp.right_shift(e, KEY_BITS)
    jj = jnp.bitwise_and(e, N_KEYS - 1)
    pc = _dot(x_ref[...], ut_ref[...])
    pf = pf_ref[...]
    for s in range(SLABS):
        got = jnp.take_along_axis(pc[:, s * N_KEYS:(s + 1) * N_KEYS], jj, axis=1)
        pf = jnp.where(ii == step * SLABS + s, got, pf)
    pf_ref[...] = pf


def _expert_hidden(eidx_ref, gate_ref, pf_ref, hm_ref):
    tn, hk = pf_ref.shape
    pf_ref[...] = _gelu_erf(pf_ref[...]) * gate_ref[...]
    sub = _iota((N_KEYS, hk), 0)

    def tokens(t, carry):
        for r in range(TOKEN_UNROLL):
            n = t * TOKEN_UNROLL + r
            e = eidx_ref[pl.ds(n, 1), :]
            hrow = pf_ref[pl.ds(n, 1), :]
            w = jnp.where(sub == jnp.right_shift(e, KEY_BITS), hrow, 0.0).astype(BF16)
            bt = jnp.where(sub == jnp.bitwise_and(e, N_KEYS - 1), 1.0, 0.0).astype(BF16)
            hn = _dot_nt(w, bt).astype(BF16).astype(F32)
            hi = pltpu.bitcast(hn[:HALF_KEYS, :], U32)
            lo = lax.shift_right_logical(pltpu.bitcast(hn[HALF_KEYS:, :], U32), U32(16))
            hm_ref[pl.ds(pl.multiple_of(n * HALF_KEYS, HALF_KEYS), HALF_KEYS), :] = hi | lo
        return carry

    lax.fori_loop(0, tn // TOKEN_UNROLL, tokens, 0)


def _expert_values(cp, hm_ref, vlo_ref, vhi_ref, o_ref):
    tn = o_ref.shape[0]
    words = [hm_ref[pl.ds(cp * SLABS + s, tn, stride=HALF_KEYS), :] for s in range(SLABS)]
    hi = [pltpu.bitcast(wd & U32(HI_HALF), F32).astype(BF16) for wd in words]
    lo = [pltpu.bitcast(lax.shift_left(wd, U32(16)), F32).astype(BF16) for wd in words]
    o_ref[...] += (_dot(jnp.concatenate(hi, axis=1), vlo_ref[...])
                   + _dot(jnp.concatenate(lo, axis=1), vhi_ref[...]))


def _peer_expert_kernel(x_ref, h_ref, eidx_ref, gate_ref, ut_ref, vlo_ref, vhi_ref, o_ref, pf_ref, hm_ref):
    step = pl.program_id(1)

    @pl.when(step == 0)
    def _():
        pf_ref[...] = jnp.zeros(pf_ref.shape, F32)

    @pl.when(step < N_CHUNKS)
    def _():
        _expert_scores(step, x_ref, ut_ref, eidx_ref, pf_ref)

    @pl.when(step == N_CHUNKS - 1)
    def _():
        _expert_hidden(eidx_ref, gate_ref, pf_ref, hm_ref)
        o_ref[...] = h_ref[...]

    @pl.when(step >= N_CHUNKS)
    def _():
        _expert_values(step - N_CHUNKS, hm_ref, vlo_ref, vhi_ref, o_ref)


SUBTILE = 128


def _peer_fused_kernel(x_ref, h_ref, q_ref, keys_ref, ut_ref, vlo_ref, vhi_ref, o_ref,
                       pf_ref, hm_ref, eidx_ref, gate_ref, et_ref, gt_ref):
    tile = pl.program_id(0)
    step = pl.program_id(1)
    hk = P_HEADS * P_TOPK
    n_sub = et_ref.shape[1]
    slot = tile % 2

    @pl.when((tile == 0) & (step == 0))
    def _():
        et_ref[...] = jnp.zeros(et_ref.shape, I32)
        gt_ref[...] = jnp.zeros(gt_ref.shape, F32)

    @pl.when(step == 0)
    def _():
        for sub in range(n_sub):
            rows = slice(sub * SUBTILE, (sub + 1) * SUBTILE)
            eidx_ref[rows, :] = et_ref[1 - slot, sub].reshape(hk, SUBTILE).T
            gate_ref[rows, :] = gt_ref[1 - slot, sub].reshape(hk, SUBTILE).T
        pf_ref[...] = jnp.zeros(pf_ref.shape, F32)

    def unit_scores(u):
        head = u % P_HEADS
        sub = u // P_HEADS
        rows = pl.ds(pl.multiple_of(sub * SUBTILE, SUBTILE), SUBTILE)
        sc = [_dot_nt(keys_ref[head, half], q_ref[2 * head + half, rows, :], HI) for half in range(2)]
        return head, sub, sc

    def unit_store(head, sub, tops):
        e, g = _pair_top16(*tops[0], *tops[1])
        et_ref[slot, sub, head] = e
        gt_ref[slot, sub, head] = g

    @pl.when(step < N_CHUNKS)
    def _():
        head, sub, sc = unit_scores(step)
        e = eidx_ref[...]
        ii = jnp.right_shift(e, KEY_BITS)
        jj = jnp.bitwise_and(e, N_KEYS - 1)
        x = x_ref[...]
        pf = pf_ref[...]
        quarter = E_CHUNK // 4
        tops = []
        for part in range(4):
            pc = _dot(x, ut_ref[:, part * quarter:(part + 1) * quarter])
            for s in range(quarter // N_KEYS):
                got = jnp.take_along_axis(pc[:, s * N_KEYS:(s + 1) * N_KEYS], jj, axis=1)
                pf = jnp.where(ii == step * SLABS + part * (quarter // N_KEYS) + s, got, pf)
            if part < 2:
                tops.append(_top16_rows(sc[part]))
            elif part == 2:
                unit_store(head, sub, tops)
        pf_ref[...] = pf

    @pl.when(step == N_CHUNKS - 1)
    def _():
        _expert_hidden(eidx_ref, gate_ref, pf_ref, hm_ref)
        o_ref[...] = h_ref[...]

    @pl.when(step >= N_CHUNKS)
    def _():
        cp = step - N_CHUNKS
        tn = o_ref.shape[0]
        ha, sa, sca = unit_scores(N_CHUNKS + 2 * cp)
        hb, sb, scb = unit_scores(N_CHUNKS + 2 * cp + 1)
        words = [hm_ref[pl.ds(cp * SLABS + s, tn, stride=HALF_KEYS), :] for s in range(SLABS)]
        hi = jnp.concatenate([pltpu.bitcast(wd & U32(HI_HALF), F32).astype(BF16) for wd in words], axis=1)
        lo = jnp.concatenate([pltpu.bitcast(lax.shift_left(wd, U32(16)), F32).astype(BF16) for wd in words], axis=1)
        half = D_MODEL // 2
        ta, tb = [], []
        acc = []
        for part in range(4):
            lhs, v_ref = (hi, vlo_ref) if part < 2 else (lo, vhi_ref)
            cols = slice((part % 2) * half, (part % 2 + 1) * half)
            acc.append(_dot(lhs, v_ref[:, cols]))
            if part < 2:
                ta.append(_top16_rows(sca[part]))
            else:
                tb.append(_top16_rows(scb[part - 2]))
        o_ref[:, :half] += acc[0] + acc[2]
        o_ref[:, half:] += acc[1] + acc[3]
        unit_store(ha, sa, ta)
        unit_store(hb, sb, tb)


def peer_fused(x2, h, q, keys, ut, v, tn):
    n = x2.shape[0]
    nt = n // tn
    hk = P_HEADS * P_TOPK
    n_sub = tn // SUBTILE
    assert n_sub * P_HEADS == N_CHUNKS + 2 * N_PAIRS
    prev = lambda w: pl.BlockSpec((tn, w), lambda i, s: (jnp.maximum(i - 1, 0), 0))
    pair = lambda s: jnp.clip(s - N_CHUNKS, 0, N_PAIRS - 1)
    return pl.pallas_call(
        _peer_fused_kernel, grid=(nt + 1, N_CHUNKS + N_PAIRS),
        out_shape=jax.ShapeDtypeStruct((n, D_MODEL), F32),
        in_specs=[prev(D_MODEL), prev(D_MODEL),
                  pl.BlockSpec((N_SUBQ, tn, P_DK), lambda i, s: (0, jnp.minimum(i, nt - 1), 0)),
                  pl.BlockSpec(keys.shape, lambda i, s: (0, 0, 0, 0)),
                  pl.BlockSpec((D_MODEL, E_CHUNK), lambda i, s: (0, jnp.minimum(s, N_CHUNKS - 1))),
                  pl.BlockSpec((E_CHUNK, D_MODEL), lambda i, s: (pair(s), 0)),
                  pl.BlockSpec((E_CHUNK, D_MODEL), lambda i, s: (pair(s) + N_PAIRS, 0))],
        out_specs=prev(D_MODEL),
        scratch_shapes=[pltpu.VMEM((tn, hk), F32), pltpu.VMEM((tn * HALF_KEYS, N_KEYS), U32),
                        pltpu.VMEM((tn, hk), I32), pltpu.VMEM((tn, hk), F32),
                        pltpu.VMEM((2, n_sub, P_HEADS, P_TOPK, SUBTILE), I32),
                        pltpu.VMEM((2, n_sub, P_HEADS, P_TOPK, SUBTILE), F32)],
        compiler_params=_params(("arbitrary", "arbitrary"), 58 * 1024 * 1024), name="peer_fused",
    )(x2, h, q, keys, ut, v, v)


def peer_experts(x2, h, eidx, gate, ut, v, tn):
    n = x2.shape[0]
    hk = P_HEADS * P_TOPK
    row = lambda w: pl.BlockSpec((tn, w), lambda i, s: (i, 0))
    pair = lambda s: jnp.clip(s - N_CHUNKS, 0, N_PAIRS - 1)
    return pl.pallas_call(
        _peer_expert_kernel, grid=(n // tn, N_CHUNKS + N_PAIRS),
        out_shape=jax.ShapeDtypeStruct((n, D_MODEL), F32),
        in_specs=[row(D_MODEL), row(D_MODEL), row(hk), row(hk),
                  pl.BlockSpec((D_MODEL, E_CHUNK), lambda i, s: (0, jnp.minimum(s, N_CHUNKS - 1))),
                  pl.BlockSpec((E_CHUNK, D_MODEL), lambda i, s: (pair(s), 0)),
                  pl.BlockSpec((E_CHUNK, D_MODEL), lambda i, s: (pair(s) + N_PAIRS, 0))],
        out_specs=row(D_MODEL),
        scratch_shapes=[pltpu.VMEM((tn, hk), F32), pltpu.VMEM((tn * HALF_KEYS, N_KEYS), U32)],
        compiler_params=_params(("parallel", "arbitrary"), 56 * 1024 * 1024), name="peer_experts",
    )(x2, h, eidx, gate, ut, v, v)


def _final_norm_kernel(h_ref, g_ref, o_ref):
    o_ref[...] = _rms(h_ref[...], g_ref[...])


def final_norm(h, g, tm):
    n = h.shape[0]
    row = pl.BlockSpec((tm, D_MODEL), lambda i: (i, 0))
    return pl.pallas_call(
        _final_norm_kernel, grid=(n // tm,), out_shape=jax.ShapeDtypeStruct(h.shape, F32),
        in_specs=[row, pl.BlockSpec(g.shape, lambda i: (0, 0))], out_specs=row,
        compiler_params=_params(("parallel",)), name="final_norm",
    )(h, g)


def _block_diag(w):
    g, a, b = w.shape
    eye = jnp.eye(g, dtype=w.dtype)
    return (eye[:, None, :, None] * w[:, :, None, :]).reshape(g * a, g * b)


def _prep_layer(W, l):
    w_in = W["w_in"][l]
    return dict(
        g1=W["norm1_g"][l][None, :],
        wa=w_in[:, OFF_AQ:OFF_AF].astype(BF16),
        wb=w_in[:, OFF_BX:OFF_CQKV].astype(BF16),
        wc=jnp.concatenate([w_in[:, OFF_CQKV:OFF_CA], w_in[:, OFF_CG:D_IN]], axis=1).astype(BF16),
        ws=jnp.concatenate([w_in[:, OFF_AF:OFF_BX], w_in[:, OFF_CA:OFF_CG]], axis=1).T,
        bf=W["b_f"][l][:, None],
        lru_cw=W["lru_conv_w"][l], lru_cb=W["lru_conv_b"][l][None, :],
        lru_wa=_block_diag(W["lru_wa"][l]), lru_ba=W["lru_ba"][l][None, :],
        lru_wx=_block_diag(W["lru_wx"][l]), lru_bx=W["lru_bx"][l][None, :],
        lru_lam=W["lru_lambda"][l][None, :],
        gdn_cw=W["gdn_conv_w"][l], gdn_alog=W["gdn_A_log"][l][:, None], gdn_dt=W["gdn_dt_bias"][l][:, None],
        gdn_ng=jnp.tile(W["gdn_norm_g"][l], C_HEADS)[None, :],
        wo=W["w_out"][l].astype(BF16), g2=W["norm2_g"][l][None, :], wq=W["peer_wq"][l].astype(BF16),
        keys=W["peer_keys"][l], ut=W["peer_u"][l].T.astype(BF16), v=W["peer_v"][l].astype(BF16),
    )


def _pair_state(s):
    nb = s.shape[0]
    s = s.reshape(nb, C_HEADS // 2, 2, C_HEAD_DIM, C_HEAD_DIM)
    eye = jnp.eye(2, dtype=s.dtype)
    return (s[:, :, :, :, None, :] * eye[None, None, :, None, :, None]).reshape(nb, C_HEADS // 2, PAIR, PAIR)


def _unpair_state(s):
    nb = s.shape[0]
    s = s.reshape(nb, C_HEADS // 2, 2, C_HEAD_DIM, 2, C_HEAD_DIM)
    return jnp.stack([s[:, :, 0, :, 0, :], s[:, :, 1, :, 1, :]], axis=2).reshape(nb, C_HEADS, C_HEAD_DIM, C_HEAD_DIM)


def _trunk(x, nb, t_real, paged, lru_h, lru_buf, gdn_s, gdn_buf, layers, final_g, cfg):
    t_rows = x.shape[1]
    n = nb * t_rows
    h = x.reshape(n, D_MODEL)
    ones_hd = _block_diag(jnp.full((C_HEADS, C_HEAD_DIM, C_HEAD_DIM), 1.0 / C_HEAD_DIM, F32))
    outs = []
    for l, lw in enumerate(layers):
        q, k, v, kb, vb, zb, zc, st = in_proj(h, lw["g1"], lw["wa"], lw["wb"], lw["wc"], lw["ws"], cfg["tm"])
        if paged is None:
            lf, c = logf_rows(st, lw["bf"], nb, True)
            ya = attn_prompt(q, kb, vb, c.reshape(A_HEADS // 2, 2, n), nb, cfg["tq"])
            n_chunks = cfg["tc_gdn"] // GDN_CHUNK
            scal = st.reshape(N_SMALL, n // GDN_CHUNK, GDN_CHUNK).transpose(1, 0, 2)
        else:
            ck, cv, clt, pt = paged
            lf, _ = logf_rows(st, lw["bf"], 1, False)
            lfn = lf.reshape(A_HEADS, nb, t_rows).transpose(1, 0, 2)
            lfn = jnp.where(jnp.arange(t_rows) < t_real, lfn, 0.0)
            lfn = jnp.pad(lfn, ((0, 0), (0, 0), (0, PAGE_SIZE - t_rows)))
            ya = attn_sample(pt, q, ck, cv, clt, k, v, lfn, l, t_real)
            n_chunks = 1
            scal = jnp.pad(st.reshape(N_SMALL, nb, t_rows).transpose(1, 0, 2),
                           ((0, 0), (0, 0), (0, GDN_CHUNK - t_rows)))
        t_valid = min(t_real, cfg["tc_lru"])
        yb, lru_ht, lru_nb = lru_mixer(zb, lru_buf[l], lru_h[l][:, None, :], lw["lru_cw"], lw["lru_cb"],
                                       lw["lru_wa"], lw["lru_ba"], lw["lru_wx"], lw["lru_bx"], lw["lru_lam"],
                                       nb, cfg["tc_lru"], t_valid)
        t_valid = min(t_real, cfg["tc_gdn"])
        yc, gdn_st, gdn_nb = gdn_mixer(zc, scal, gdn_buf[l], _pair_state(gdn_s[l]), lw["gdn_cw"], lw["gdn_alog"],
                                       lw["gdn_dt"], lw["gdn_ng"], ones_hd, nb, cfg["tc_gdn"], t_valid, n_chunks)
        hmid, x2, qp = out_proj(h, ya, yb, yc, lw["wo"], lw["g2"], lw["wq"], cfg["tm"])
        if cfg["fused_peer"]:
            h = peer_fused(x2, hmid, qp, lw["keys"], lw["ut"], lw["v"], cfg["tn_peer"])
        else:
            eidx, gate = peer_topk(qp, lw["keys"], cfg["tn_topk"])
            h = peer_experts(x2, hmid, eidx, gate, lw["ut"], lw["v"], cfg["tn_peer"])
        outs.append((k.reshape(nb, t_rows, A_HEADS, A_HEAD_DIM)[:, :t_real],
                     v.reshape(nb, t_rows, A_HEADS, A_HEAD_DIM)[:, :t_real],
                     lf.reshape(A_HEADS, nb, t_rows).transpose(1, 2, 0)[:, :t_real],
                     lru_ht[:, 0, :], lru_nb, _unpair_state(gdn_st), gdn_nb))
    y = final_norm(h, final_g[None, :], cfg["tm"]).reshape(nb, t_rows, D_MODEL)[:, :t_real]
    return y, tuple(jnp.stack([o[i] for o in outs], axis=0) for i in range(len(outs[0])))


PROMPT_CFG = dict(tm=512, tq=512, tc_lru=512, tc_gdn=256, tn_topk=None, tn_peer=512, fused_peer=True)
SAMPLE_CFG = dict(tm=256, tq=None, tc_lru=SAMPLE_ROWS, tc_gdn=SAMPLE_ROWS, tn_topk=128, tn_peer=256, fused_peer=False)


def kernel(x_prompt, x_sample, cache_k, cache_v, cache_logf, page_table, state_lru_h, state_lru_conv,
           state_gdn_S, state_gdn_conv, norm1_g, w_in, b_f, lru_conv_w, lru_conv_b, lru_wa, lru_ba,
           lru_wx, lru_bx, lru_lambda, gdn_conv_w, gdn_A_log, gdn_dt_bias, gdn_norm_g, w_out, norm2_g,
           peer_wq, peer_keys, peer_u, peer_v, final_norm_g):
    W = dict(norm1_g=norm1_g, w_in=w_in, b_f=b_f, lru_conv_w=lru_conv_w, lru_conv_b=lru_conv_b, lru_wa=lru_wa,
             lru_ba=lru_ba, lru_wx=lru_wx, lru_bx=lru_bx, lru_lambda=lru_lambda, gdn_conv_w=gdn_conv_w,
             gdn_A_log=gdn_A_log, gdn_dt_bias=gdn_dt_bias, gdn_norm_g=gdn_norm_g, w_out=w_out, norm2_g=norm2_g,
             peer_wq=peer_wq, peer_keys=peer_keys, peer_u=peer_u, peer_v=peer_v)
    depth = w_in.shape[0]
    layers = [_prep_layer(W, l) for l in range(depth)]
    bp, tp = x_prompt.shape[:2]
    bs, ts = x_sample.shape[:2]

    ck = cache_k.transpose(0, 1, 3, 4, 2)
    cv = cache_v.transpose(0, 1, 3, 4, 2)
    clt = cache_logf.transpose(0, 1, 3, 2)
    xs = jnp.pad(x_sample, ((0, 0), (0, SAMPLE_ROWS - ts), (0, 0)))
    ys, ss = _trunk(xs, bs, ts, (ck, cv, clt, page_table), state_lru_h, state_lru_conv, state_gdn_S,
                    state_gdn_conv, layers, final_norm_g, SAMPLE_CFG)

    zeros = lambda *s: jnp.zeros((depth, bp) + s, F32)
    yp, sp = _trunk(x_prompt, bp, tp, None, zeros(B_WIDTH), zeros(CONV_W - 1, B_WIDTH),
                    zeros(C_HEADS, C_HEAD_DIM, C_HEAD_DIM), zeros(CONV_W - 1, 3 * C_WIDTH),
                    layers, final_norm_g, PROMPT_CFG)

    return (yp, ys, sp[0], sp[1], sp[2], ss[0], ss[1], ss[2], sp[3], ss[3], sp[4], ss[4], sp[5], ss[5], sp[6], ss[6])
```

```python
import functools

import jax
import jax.numpy as jnp
from jax import lax
from jax.experimental import pallas as pl
from jax.experimental.pallas import tpu as pltpu

F32 = jnp.float32
BF16 = jnp.bfloat16
I32 = jnp.int32
HI = lax.Precision.HIGHEST

D_MODEL = 1024
A_HEADS = 8
A_HEAD_DIM = 64
A_WIDTH = A_HEADS * A_HEAD_DIM
B_WIDTH = 256
B_BLOCKS = 4
LRU_C = 8.0
C_HEADS = 4
C_HEAD_DIM = 64
C_WIDTH = C_HEADS * C_HEAD_DIM
GDN_CHUNK = 64
CONV_W = 4
P_HEADS = 8
P_DK = 128
N_KEYS = 128
N_EXPERTS = N_KEYS * N_KEYS
P_TOPK = 16
PAGE_SIZE = 128
EPS = 1e-6
SAMPLE_ROWS = 8
N_SMALL = 16

OFF_AQ = 0
OFF_AF = OFF_AQ + 3 * A_WIDTH
OFF_BX = OFF_AF + A_HEADS
OFF_CQKV = OFF_BX + 2 * B_WIDTH
OFF_CA = OFF_CQKV + 3 * C_WIDTH
OFF_CG = OFF_CA + 2 * C_HEADS
D_IN = OFF_CG + C_WIDTH

VMEM_LIMIT = 48 * 1024 * 1024


def _params(sem, vmem=VMEM_LIMIT):
    return pltpu.CompilerParams(dimension_semantics=sem, vmem_limit_bytes=vmem)


def _dot(a, b, precision=None):
    return jnp.dot(a, b, precision=precision, preferred_element_type=F32)


def _dot_nt(a, b, precision=None):
    return lax.dot_general(a, b, (((1,), (1,)), ((), ())), precision=precision, preferred_element_type=F32)


def _dot_tn(a, b, precision=None):
    return lax.dot_general(a, b, (((0,), (0,)), ((), ())), precision=precision, preferred_element_type=F32)


_DOTS = {"nn": _dot, "nt": _dot_nt, "tn": _dot_tn}


def _split_bf16(a):
    hi = a.astype(BF16)
    return hi, (a - hi.astype(F32)).astype(BF16)


def _mm(a, b, prec, kind="nn"):
    dot = _DOTS[kind]
    if prec == "bf":
        return dot(a.astype(BF16), b.astype(BF16))
    if prec == "x2l":
        ah, al = _split_bf16(a)
        bb = b.astype(BF16)
        return dot(ah, bb) + dot(al, bb)
    if prec == "x3":
        ah, al = _split_bf16(a)
        bh, bl = _split_bf16(b)
        return dot(ah, bh) + (dot(ah, bl) + dot(al, bh))
    return dot(a, b, HI)


def _rms(x, g):
    return x * lax.rsqrt(jnp.mean(x * x, axis=-1, keepdims=True) + EPS) * g


def _gelu_erf(x):
    return 0.5 * x * (1.0 + lax.erf(x * 0.7071067811865476))


def _iota(shape, dim):
    return lax.broadcasted_iota(I32, shape, dim)


def _inproj_kernel(x_ref, g_ref, wa_ref, wb_ref, wc_ref, ws_ref,
                   q_ref, k_ref, v_ref, kb_ref, vb_ref, zb_ref, zc_ref, st_ref):
    hn = _rms(x_ref[...], g_ref[...])
    hb = hn.astype(BF16)
    za = _dot(hb, wa_ref[...])
    q_ref[...] = (za[:, :A_WIDTH] * (A_HEAD_DIM ** -0.5)).astype(BF16)
    k = za[:, A_WIDTH:2 * A_WIDTH]
    v = za[:, 2 * A_WIDTH:]
    k_ref[...] = k
    v_ref[...] = v
    kb_ref[...] = k.astype(BF16)
    vb_ref[...] = v.astype(BF16)
    zb_ref[...] = _dot(hb, wb_ref[...])
    zc_ref[...] = _dot(hb, wc_ref[...])
    st_ref[...] = _dot_nt(ws_ref[...], hn, HI)


def in_proj(h, g, wa, wb, wc, ws, tm):
    n = h.shape[0]
    grid = (n // tm,)
    row = lambda w: pl.BlockSpec((tm, w), lambda i: (i, 0))
    full = lambda a: pl.BlockSpec(a.shape, lambda i: (0,) * a.ndim)
    out_shape = [
        jax.ShapeDtypeStruct((n, A_WIDTH), BF16),
        jax.ShapeDtypeStruct((n, A_WIDTH), F32),
        jax.ShapeDtypeStruct((n, A_WIDTH), F32),
        jax.ShapeDtypeStruct((n, A_WIDTH), BF16),
        jax.ShapeDtypeStruct((n, A_WIDTH), BF16),
        jax.ShapeDtypeStruct((n, 2 * B_WIDTH), F32),
        jax.ShapeDtypeStruct((n, 4 * C_WIDTH), F32),
        jax.ShapeDtypeStruct((N_SMALL, n), F32),
    ]
    out_specs = [row(A_WIDTH)] * 5 + [row(2 * B_WIDTH), row(4 * C_WIDTH),
                                      pl.BlockSpec((N_SMALL, tm), lambda i: (0, i))]
    return pl.pallas_call(
        _inproj_kernel, grid=grid, out_shape=out_shape,
        in_specs=[row(D_MODEL), full(g), full(wa), full(wb), full(wc), full(ws)],
        out_specs=out_specs, compiler_params=_params(("parallel",)), name="in_proj",
    )(h, g, wa, wb, wc, ws)


CUMSUM_CHUNK = 512


def _logf_kernel(st_ref, bf_ref, lf_ref, c_ref, *, cumsum):
    lf = jax.nn.log_sigmoid(st_ref[0:A_HEADS, :] + bf_ref[...])
    lf_ref[...] = lf
    if not cumsum:
        c_ref[...] = lf
        return
    t = lf.shape[1]
    ch = min(CUMSUM_CHUNK, t)
    tri = (_iota((ch, ch), 0) <= _iota((ch, ch), 1)).astype(F32)
    carry = jnp.zeros((A_HEADS, 1), F32)
    for j in range(t // ch):
        cs = _dot(lf[:, j * ch:(j + 1) * ch], tri, HI) + carry
        c_ref[:, j * ch:(j + 1) * ch] = cs
        carry = cs[:, ch - 1:ch]


def logf_rows(st, bf, nb, cumsum):
    n = st.shape[1]
    t = n // nb
    spec_in = pl.BlockSpec((N_SMALL, t), lambda b: (0, b))
    spec_out = pl.BlockSpec((A_HEADS, t), lambda b: (0, b))
    return pl.pallas_call(
        functools.partial(_logf_kernel, cumsum=cumsum), grid=(nb,),
        out_shape=[jax.ShapeDtypeStruct((A_HEADS, n), F32)] * 2,
        in_specs=[spec_in, pl.BlockSpec(bf.shape, lambda b: (0, 0))],
        out_specs=[spec_out, spec_out], compiler_params=_params(("parallel",)), name="logf",
    )(st, bf)


def _attn_prompt_kernel(qi_ref, ki_ref, q_ref, k_ref, v_ref, c_ref, o_ref, m_ref, l_ref, acc_ref):
    step = pl.program_id(2)
    qi = qi_ref[step]
    ki = ki_ref[step]
    tq, tk = q_ref.shape[0], k_ref.shape[0]
    w = 2 * A_HEAD_DIM
    lane = _iota((1, w), 1)

    @pl.when(ki == 0)
    def _():
        m_ref[...] = jnp.full(m_ref.shape, -jnp.inf, F32)
        l_ref[...] = jnp.zeros(l_ref.shape, F32)
        acc_ref[...] = jnp.zeros(acc_ref.shape, F32)

    def block(masked):
        q = q_ref[...]
        k = k_ref[...]
        v = v_ref[...]
        for h in range(2):
            head = (lane >= A_HEAD_DIM) if h else (lane < A_HEAD_DIM)
            s = _dot_nt(jnp.where(head, q, jnp.zeros_like(q)), k) - c_ref[0, h:h + 1, :]
            if masked:
                s = jnp.where(_iota((tq, tk), 1) <= _iota((tq, tk), 0), s, -jnp.inf)
            m_prev = m_ref[h]
            m_new = jnp.maximum(m_prev, jnp.max(s, axis=1, keepdims=True))
            alpha = jnp.exp(m_prev - m_new)
            p = jnp.exp(s - jnp.concatenate([m_new] * (tk // w), axis=1))
            l_ref[h] = alpha * l_ref[h] + jnp.sum(p, axis=1, keepdims=True)
            acc_ref[h] = alpha * acc_ref[h] + _dot(p.astype(BF16), v)
            m_ref[h] = m_new

    @pl.when(ki < qi)
    def _():
        block(False)

    @pl.when(ki == qi)
    def _():
        block(True)
        o0 = acc_ref[0] / l_ref[0]
        o1 = acc_ref[1] / l_ref[1]
        o_ref[...] = jnp.where(lane < A_HEAD_DIM, o0, o1).astype(o_ref.dtype)


def attn_prompt(q, kb, vb, c, nb, tq):
    n = q.shape[0]
    t = n // nb
    nq = t // tq
    w = 2 * A_HEAD_DIM
    pairs = [(i, j) for i in range(nq) for j in range(i + 1)]
    qi_list = jnp.asarray([i for i, _ in pairs], I32)
    ki_list = jnp.asarray([j for _, j in pairs], I32)
    qspec = pl.BlockSpec((tq, w), lambda b, p, s, qi, ki: (b * nq + qi[s], p))
    kspec = pl.BlockSpec((tq, w), lambda b, p, s, qi, ki: (b * nq + ki[s], p))
    cspec = pl.BlockSpec((1, 2, tq), lambda b, p, s, qi, ki: (p, 0, b * nq + ki[s]))
    grid_spec = pltpu.PrefetchScalarGridSpec(
        num_scalar_prefetch=2, grid=(nb, A_HEADS // 2, len(pairs)),
        in_specs=[qspec, kspec, kspec, cspec], out_specs=qspec,
        scratch_shapes=[pltpu.VMEM((2, tq, w), F32), pltpu.VMEM((2, tq, w), F32), pltpu.VMEM((2, tq, w), F32)])
    return pl.pallas_call(
        _attn_prompt_kernel, grid_spec=grid_spec,
        out_shape=jax.ShapeDtypeStruct((n, A_WIDTH), BF16),
        compiler_params=_params(("parallel", "parallel", "arbitrary")), name="attn_prompt",
    )(qi_list, ki_list, q, kb, vb, c)


PAGE_GROUP = 8


def _attn_sample_kernel(pt_ref, q_ref, kn_ref, vn_ref, ln_ref, ck_hbm, cv_hbm, cl_hbm, o_ref,
                        kbuf, vbuf, lbuf, c_ref, qh_ref, m_ref, l_ref, acc_ref, ksem, vsem, lsem,
                        *, layer, n_pages, t_new):
    b = pl.program_id(0)
    nr = A_HEADS * SAMPLE_ROWS
    hd = A_HEAD_DIM
    n_groups = n_pages // PAGE_GROUP

    def kv_copies(g, slot):
        cps = []
        for i in range(PAGE_GROUP):
            pg = pt_ref[b, g * PAGE_GROUP + i]
            cps.append(pltpu.make_async_copy(ck_hbm.at[layer, pg], kbuf.at[slot, i], ksem.at[slot]))
            cps.append(pltpu.make_async_copy(cv_hbm.at[layer, pg], vbuf.at[slot, i], vsem.at[slot]))
        return cps

    def lf_copy(j):
        return pltpu.make_async_copy(cl_hbm.at[layer, pt_ref[b, j]], lbuf.at[j], lsem)

    for j in range(n_pages):
        lf_copy(j).start()
    for cp in kv_copies(0, 0):
        cp.start()

    q = q_ref[...].astype(F32)
    for h in range(A_HEADS):
        qh_ref[h] = q[:, h * hd:(h + 1) * hd]
    m_ref[...] = jnp.full(m_ref.shape, -jnp.inf, F32)
    l_ref[...] = jnp.zeros(l_ref.shape, F32)
    acc_ref[...] = jnp.zeros(acc_ref.shape, F32)

    for j in range(n_pages):
        lf_copy(j).wait()
    tri = (_iota((PAGE_SIZE, PAGE_SIZE), 0) <= _iota((PAGE_SIZE, PAGE_SIZE), 1)).astype(F32)
    nrow = n_pages * A_HEADS
    cs = _dot(lbuf[...].reshape(nrow, PAGE_SIZE), tri, HI)
    totals = jnp.broadcast_to(cs[:, PAGE_SIZE - 1:PAGE_SIZE], (nrow, PAGE_SIZE))
    r, cc = _iota((nrow, nrow), 0), _iota((nrow, nrow), 1)
    earlier = ((cc // A_HEADS < r // A_HEADS) & (cc % A_HEADS == r % A_HEADS)).astype(F32)
    c_all = cs + _dot(earlier, totals, HI)
    c_ref[...] = c_all.reshape(n_pages, A_HEADS, PAGE_SIZE)
    carry = c_all[nrow - A_HEADS:, PAGE_SIZE - 1:PAGE_SIZE]

    def update(s, weighted):
        m_prev = m_ref[...]
        m_new = jnp.maximum(m_prev, jnp.max(s, axis=1, keepdims=True))
        alpha = jnp.exp(m_prev - m_new)
        p = jnp.exp(s - m_new)
        l_ref[...] = alpha * l_ref[...] + jnp.sum(p, axis=1, keepdims=True)
        pv = jnp.concatenate([weighted(h, p[h * SAMPLE_ROWS:(h + 1) * SAMPLE_ROWS, :].astype(BF16))
                              for h in range(A_HEADS)], axis=0)
        acc_ref[...] = alpha * acc_ref[...] + pv
        m_ref[...] = m_new

    def group(g, carry_):
        slot = g % 2

        @pl.when(g + 1 < n_groups)
        def _():
            for cp in kv_copies(g + 1, 1 - slot):
                cp.start()

        for cp in kv_copies(g, slot):
            cp.wait()
        kb = kbuf.at[slot]
        vb = vbuf.at[slot]
        tiles = []
        for i in range(PAGE_GROUP):
            c = c_ref[g * PAGE_GROUP + i]
            tiles.append(jnp.concatenate(
                [_dot(qh_ref[h].astype(BF16), kb[i, h].astype(BF16)) - c[h:h + 1, :] for h in range(A_HEADS)], axis=0))

        def weighted(h, ph):
            out = _dot_nt(ph[:, 0:PAGE_SIZE], vb[0, h].astype(BF16))
            for i in range(1, PAGE_GROUP):
                out = out + _dot_nt(ph[:, i * PAGE_SIZE:(i + 1) * PAGE_SIZE], vb[i, h].astype(BF16))
            return out

        update(jnp.concatenate(tiles, axis=1), weighted)
        return carry_

    lax.fori_loop(0, n_groups, group, 0)

    pad = jnp.zeros((PAGE_SIZE - SAMPLE_ROWS, hd), F32)
    rows = lambda ref, h: jnp.concatenate([ref[:, h * hd:(h + 1) * hd], pad], axis=0).astype(BF16)
    c_new = _dot(ln_ref[...], tri, HI) + carry
    visible = _iota((nr, PAGE_SIZE), 1) <= _iota((nr, PAGE_SIZE), 0) % SAMPLE_ROWS
    s_new = jnp.concatenate([_dot_nt(qh_ref[h].astype(BF16), rows(kn_ref, h)) - c_new[h:h + 1, :]
                             for h in range(A_HEADS)], axis=0)
    update(jnp.where(visible, s_new, -jnp.inf), lambda h, ph: _dot(ph, rows(vn_ref, h)))

    o = acc_ref[...] / l_ref[...]
    o = jnp.concatenate([o[h * SAMPLE_ROWS:(h + 1) * SAMPLE_ROWS, :] for h in range(A_HEADS)], axis=1)
    o_ref[...] = jnp.where(_iota(o.shape, 0) < t_new, o, 0.0).astype(o_ref.dtype)


def attn_sample(page_table, q, cache_k, cache_v, cache_lt, kn, vn, lfn, layer, t_new):
    nb, n_pages = page_table.shape
    assert n_pages % PAGE_GROUP == 0
    nr = A_HEADS * SAMPLE_ROWS
    seq = lambda w: pl.BlockSpec((SAMPLE_ROWS, w), lambda b, pt: (b, 0))
    hbm = pl.BlockSpec(memory_space=pl.ANY)
    page_buf = pltpu.VMEM((2, PAGE_GROUP, A_HEADS, A_HEAD_DIM, PAGE_SIZE), F32)
    grid_spec = pltpu.PrefetchScalarGridSpec(
        num_scalar_prefetch=1, grid=(nb,),
        in_specs=[seq(A_WIDTH), seq(A_WIDTH), seq(A_WIDTH),
                  pl.BlockSpec((None, A_HEADS, PAGE_SIZE), lambda b, pt: (b, 0, 0)), hbm, hbm, hbm],
        out_specs=seq(A_WIDTH),
        scratch_shapes=[page_buf, page_buf,
                        pltpu.VMEM((n_pages, A_HEADS, PAGE_SIZE), F32), pltpu.VMEM((n_pages, A_HEADS, PAGE_SIZE), F32),
                        pltpu.VMEM((A_HEADS, SAMPLE_ROWS, A_HEAD_DIM), F32), pltpu.VMEM((nr, 1), F32),
                        pltpu.VMEM((nr, 1), F32), pltpu.VMEM((nr, A_HEAD_DIM), F32),
                        pltpu.SemaphoreType.DMA((2,)), pltpu.SemaphoreType.DMA((2,)), pltpu.SemaphoreType.DMA(())])
    return pl.pallas_call(
        functools.partial(_attn_sample_kernel, layer=layer, n_pages=n_pages, t_new=t_new), grid_spec=grid_spec,
        out_shape=jax.ShapeDtypeStruct(q.shape, BF16),
        compiler_params=_params(("arbitrary",)), name="attn_sample",
    )(page_table, q, kn, vn, lfn, cache_k, cache_v, cache_lt)


CONV_PAD = 8


def _conv_block(xp_ref, x, w_ref, tc):
    xp_ref[CONV_PAD:CONV_PAD + tc, :] = x
    y = xp_ref[CONV_PAD - 3:CONV_PAD - 3 + tc, :] * w_ref[0:1, :]
    for i in range(1, CONV_W):
        y = y + xp_ref[CONV_PAD - 3 + i:CONV_PAD - 3 + i + tc, :] * w_ref[i:i + 1, :]
    return y


def _lru_kernel(z_ref, buf_ref, h0_ref, cw_ref, cb_ref, wa_ref, ba_ref, wx_ref, bx_ref, lam_ref,
                y_ref, ht_ref, nb_ref, xp_ref, hc_ref, *, t_valid):
    ti = pl.program_id(1)
    tc = z_ref.shape[0]

    @pl.when(ti == 0)
    def _():
        xp_ref[CONV_PAD - 3:CONV_PAD, :] = buf_ref[...]
        hc_ref[...] = h0_ref[...]

    x = z_ref[:, :B_WIDTH]
    xc = _conv_block(xp_ref, x, cw_ref, tc) + cb_ref[...]
    r = jax.nn.sigmoid(_dot(xc, wa_ref[...], HI) + ba_ref[...])
    ig = jax.nn.sigmoid(_dot(xc, wx_ref[...], HI) + bx_ref[...])
    log_a = -LRU_C * r * jax.nn.softplus(-lam_ref[...])
    a = jnp.exp(log_a)
    b = jnp.sqrt(1.0 - jnp.exp(2.0 * log_a)) * (ig * xc)
    row = _iota((tc, B_WIDTH), 0)
    d = 1
    while d < tc:
        keep = row >= d
        a_prev = jnp.where(keep, pltpu.roll(a, d, 0), 1.0)
        b_prev = jnp.where(keep, pltpu.roll(b, d, 0), 0.0)
        b = a * b_prev + b
        a = a * a_prev
        d *= 2
    hs = a * hc_ref[...] + b
    y_ref[...] = (hs * jax.nn.gelu(z_ref[:, B_WIDTH:])).astype(y_ref.dtype)
    hc_ref[...] = hs[t_valid - 1:t_valid, :]
    xp_ref[CONV_PAD - 3:CONV_PAD, :] = xp_ref[CONV_PAD + t_valid - 3:CONV_PAD + t_valid, :]

    @pl.when(ti == pl.num_programs(1) - 1)
    def _():
        ht_ref[...] = hc_ref[...]
        nb_ref[...] = xp_ref[CONV_PAD - 3:CONV_PAD, :]


def lru_mixer(zb, buf0, h0, cw, cb, wa, ba, wx, bx, lam, nb, tc, t_valid):
    n = zb.shape[0]
    nt = n // nb // tc
    full = lambda a: pl.BlockSpec(a.shape, lambda b, t: (0,) * a.ndim)
    per_b = lambda r: pl.BlockSpec((None, r, B_WIDTH), lambda b, t: (b, 0, 0))
    return pl.pallas_call(
        functools.partial(_lru_kernel, t_valid=t_valid), grid=(nb, nt),
        out_shape=[jax.ShapeDtypeStruct((n, B_WIDTH), BF16), jax.ShapeDtypeStruct((nb, 1, B_WIDTH), F32),
                   jax.ShapeDtypeStruct((nb, CONV_W - 1, B_WIDTH), F32)],
        in_specs=[pl.BlockSpec((tc, 2 * B_WIDTH), lambda b, t: (b * nt + t, 0)), per_b(CONV_W - 1), per_b(1),
                  full(cw), full(cb), full(wa), full(ba), full(wx), full(bx), full(lam)],
        out_specs=[pl.BlockSpec((tc, B_WIDTH), lambda b, t: (b * nt + t, 0)), per_b(1), per_b(CONV_W - 1)],
        scratch_shapes=[pltpu.VMEM((CONV_PAD + tc, B_WIDTH), F32), pltpu.VMEM((1, B_WIDTH), F32)],
        compiler_params=_params(("parallel", "arbitrary")), name="lru",
    )(zb, buf0, h0, cw, cb, wa, ba, wx, bx, lam)


PAIR = 2 * C_HEAD_DIM
GDN_INV_PREC = "bf"
GDN_PREC = "bf"
GDN_LOCKSTEP = 8


def _stack_heads(x, lo_mask):
    return jnp.concatenate([jnp.where(lo_mask, x, 0.0), jnp.where(lo_mask, 0.0, x)], axis=0)


def _row_to_col(row, eye):
    return jnp.sum(jnp.where(eye, row, 0.0), axis=1, keepdims=True)


def _gdn_kernel(z_ref, sc_ref, buf_ref, s0_ref, cw_ref, alog_ref, dt_ref, ng_ref, ones_hd_ref,
                y_ref, st_ref, nb_ref, xp_ref, s_ref, o_ref, ys_ref, wy_ref, eg_ref, *, t_valid, n_chunks):
    ti = pl.program_id(1)
    tc = z_ref.shape[0]
    c = GDN_CHUNK
    qkv_w = 3 * C_WIDTH

    @pl.when(ti == 0)
    def _():
        xp_ref[CONV_PAD - 3:CONV_PAD, :] = buf_ref[...]
        s_ref[...] = s0_ref[...]

    y = jax.nn.silu(_conv_block(xp_ref, z_ref[:, :qkv_w], cw_ref, tc))
    xp_ref[CONV_PAD - 3:CONV_PAD, :] = xp_ref[CONV_PAD + t_valid - 3:CONV_PAD + t_valid, :]
    padded = tc < n_chunks * c
    if padded:
        y = jnp.concatenate([y, jnp.zeros((n_chunks * c - tc, qkv_w), F32)], axis=0)
    ones_hd = ones_hd_ref[...]

    def l2n(x):
        return x * lax.rsqrt(_mm(x * x, ones_hd, "x2l") * C_HEAD_DIM + EPS)

    ys_ref[:, :C_WIDTH] = l2n(y[:, :C_WIDTH]) * (C_HEAD_DIM ** -0.5)
    ys_ref[:, C_WIDTH:2 * C_WIDTH] = l2n(y[:, C_WIDTH:2 * C_WIDTH])
    ys_ref[:, 2 * C_WIDTH:] = y[:, 2 * C_WIDTH:]

    r128 = _iota((PAIR, PAIR), 0)
    c128 = _iota((PAIR, PAIR), 1)
    same = (r128 // c) == (c128 // c)
    eye = r128 == c128
    incl = same & (c128 <= r128)
    strict = same & (c128 < r128)
    tri_bd = (same & (r128 <= c128)).astype(F32)
    last_sel = same & (c128 % c == c - 1)
    lo_lane = _iota((1, PAIR), 1) < c
    lane_t = _iota((1, PAIR), 1) % c
    ident = eye.astype(F32)

    def prepare(ci, p):
        sc = sc_ref[ci]
        g4 = -jnp.exp(alog_ref[...]) * jax.nn.softplus(sc[8:12, :] + dt_ref[...])
        b4 = jax.nn.sigmoid(sc[12:16, :])
        g_row = jnp.concatenate([g4[2 * p:2 * p + 1, :], g4[2 * p + 1:2 * p + 2, :]], axis=1)
        b_row = jnp.concatenate([b4[2 * p:2 * p + 1, :], b4[2 * p + 1:2 * p + 2, :]], axis=1)
        if padded:
            ok = lane_t < t_valid
            g_row = jnp.where(ok, g_row, 0.0)
            b_row = jnp.where(ok, b_row, 0.0)
        gc_row = _dot(jnp.broadcast_to(g_row, (8, PAIR)), tri_bd, HI)[0:1, :]
        gc_col = _row_to_col(gc_row, eye)
        b_col = _row_to_col(b_row, eye)
        gl_col = jnp.sum(jnp.where(last_sel, gc_row, 0.0), axis=1, keepdims=True)
        decay = jnp.where(incl, jnp.exp(jnp.where(incl, gc_col - gc_row, 0.0)), 0.0)
        rows = slice(ci * c, (ci + 1) * c)
        q_st = _stack_heads(ys_ref[rows, p * PAIR:(p + 1) * PAIR], lo_lane)
        k_st = _stack_heads(ys_ref[rows, C_WIDTH + p * PAIR:C_WIDTH + (p + 1) * PAIR], lo_lane)
        v_st = _stack_heads(ys_ref[rows, 2 * C_WIDTH + p * PAIR:2 * C_WIDTH + (p + 1) * PAIR], lo_lane)
        kb_st = k_st * b_col
        wy_ref[3, ci, p] = q_st * jnp.exp(gc_col)
        wy_ref[4, ci, p] = k_st * jnp.exp(gl_col - gc_col)
        eg_ref[ci, p] = jnp.exp(gl_col)
        return dict(decay=decay, q=q_st, k=k_st, kb=kb_st, vb=v_st * b_col, kbe=kb_st * jnp.exp(gc_col))

    chains = [(ci, p) for ci in range(n_chunks) for p in range(C_HEADS // 2)]
    for g0 in range(0, len(chains), GDN_LOCKSTEP):
        group = chains[g0:g0 + GDN_LOCKSTEP]
        st = [prepare(ci, p) for ci, p in group]
        lmat = [jnp.where(strict, _mm(d["kb"], d["k"], GDN_PREC, "nt") * d["decay"], 0.0) for d in st]
        tinv = [ident - m for m in lmat]
        pw = lmat
        for _ in range(5):
            pw = [_mm(x, x, GDN_INV_PREC) for x in pw]
            tinv = [t + _mm(t, x, GDN_INV_PREC) for t, x in zip(tinv, pw)]
        for (ci, p), d, t in zip(group, st, tinv):
            wy_ref[0, ci, p] = _mm(t, d["vb"], GDN_INV_PREC)
            wy_ref[1, ci, p] = _mm(t, d["kbe"], GDN_INV_PREC)
            wy_ref[2, ci, p] = _mm(d["q"], d["k"], GDN_PREC, "nt") * d["decay"]

    for ci in range(n_chunks):
        for p in range(C_HEADS // 2):
            s = s_ref[p]
            v_new = wy_ref[0, ci, p] - _mm(wy_ref[1, ci, p], s, GDN_PREC)
            o_st = _mm(wy_ref[3, ci, p], s, GDN_PREC) + _mm(wy_ref[2, ci, p], v_new, GDN_PREC)
            o_ref[ci * c:(ci + 1) * c, p * PAIR:(p + 1) * PAIR] = o_st[:c, :] + o_st[c:, :]
            s_ref[p] = s * eg_ref[ci, p] + _mm(wy_ref[4, ci, p], v_new, GDN_PREC, "tn")

    o = o_ref[0:tc, :]
    ms = _mm(o * o, ones_hd, "x2l")
    on = o * lax.rsqrt(ms + EPS) * ng_ref[...]
    y_ref[...] = (on * jax.nn.silu(z_ref[:, qkv_w:])).astype(y_ref.dtype)

    @pl.when(ti == pl.num_programs(1) - 1)
    def _():
        st_ref[...] = s_ref[...]
        nb_ref[...] = xp_ref[CONV_PAD - 3:CONV_PAD, :]


def gdn_mixer(zc, scal, buf0, s0, cw, alog, dt, ng, ones_hd, nb, tc, t_valid, n_chunks):
    n = zc.shape[0]
    nt = n // nb // tc
    qkv_w = 3 * C_WIDTH
    full = lambda a: pl.BlockSpec(a.shape, lambda b, t: (0,) * a.ndim)
    return pl.pallas_call(
        functools.partial(_gdn_kernel, t_valid=t_valid, n_chunks=n_chunks), grid=(nb, nt),
        out_shape=[jax.ShapeDtypeStruct((n, C_WIDTH), BF16),
                   jax.ShapeDtypeStruct((nb, C_HEADS // 2, PAIR, PAIR), F32),
                   jax.ShapeDtypeStruct((nb, CONV_W - 1, qkv_w), F32)],
        in_specs=[pl.BlockSpec((tc, 4 * C_WIDTH), lambda b, t: (b * nt + t, 0)),
                  pl.BlockSpec((n_chunks, N_SMALL, GDN_CHUNK), lambda b, t: (b * nt + t, 0, 0)),
                  pl.BlockSpec((None, CONV_W - 1, qkv_w), lambda b, t: (b, 0, 0)),
                  pl.BlockSpec((None, C_HEADS // 2, PAIR, PAIR), lambda b, t: (b, 0, 0, 0)),
                  full(cw), full(alog), full(dt), full(ng), full(ones_hd)],
        out_specs=[pl.BlockSpec((tc, C_WIDTH), lambda b, t: (b * nt + t, 0)),
                   pl.BlockSpec((None, C_HEADS // 2, PAIR, PAIR), lambda b, t: (b, 0, 0, 0)),
                   pl.BlockSpec((None, CONV_W - 1, qkv_w), lambda b, t: (b, 0, 0))],
        scratch_shapes=[pltpu.VMEM((CONV_PAD + tc, qkv_w), F32),
                        pltpu.VMEM((C_HEADS // 2, PAIR, PAIR), F32),
                        pltpu.VMEM((n_chunks * GDN_CHUNK, C_WIDTH), F32),
                        pltpu.VMEM((n_chunks * GDN_CHUNK, qkv_w), F32),
                        pltpu.VMEM((5, n_chunks, C_HEADS // 2, PAIR, PAIR), F32),
                        pltpu.VMEM((n_chunks, C_HEADS // 2, PAIR, 1), F32)],
        compiler_params=_params(("parallel", "arbitrary")), name="gdn",
    )(zc, scal, buf0, s0, cw, alog, dt, ng, ones_hd)


def _outproj_kernel(h_ref, ya_ref, yb_ref, yc_ref, wo_ref, g_ref, wq_ref, hn_ref, x2_ref, q_ref):
    mix = jnp.concatenate([ya_ref[...], yb_ref[...], yc_ref[...]], axis=1)
    h = h_ref[...] + _dot(mix, wo_ref[...])
    hn_ref[...] = h
    x2 = _rms(h, g_ref[...]).astype(BF16)
    x2_ref[...] = x2
    q = _dot(x2, wq_ref[...])
    for hp in range(N_SUBQ):
        q_ref[hp] = q[:, hp * P_DK:(hp + 1) * P_DK]


N_SUBQ = 2 * P_HEADS


def out_proj(h, ya, yb, yc, wo, g, wq, tm):
    n = h.shape[0]
    row = lambda w: pl.BlockSpec((tm, w), lambda i: (i, 0))
    full = lambda a: pl.BlockSpec(a.shape, lambda i: (0,) * a.ndim)
    return pl.pallas_call(
        _outproj_kernel, grid=(n // tm,),
        out_shape=[jax.ShapeDtypeStruct((n, D_MODEL), F32), jax.ShapeDtypeStruct((n, D_MODEL), BF16),
                   jax.ShapeDtypeStruct((N_SUBQ, n, P_DK), F32)],
        in_specs=[row(D_MODEL), row(A_WIDTH), row(B_WIDTH), row(C_WIDTH), full(wo), full(g), full(wq)],
        out_specs=[row(D_MODEL), row(D_MODEL), pl.BlockSpec((N_SUBQ, tm, P_DK), lambda i: (0, i, 0))],
        compiler_params=_params(("parallel",)), name="out_proj",
    )(h, ya, yb, yc, wo, g, wq)


def _top16_rows(items):
    state = [s for s, _ in items]
    rows = [_iota(s.shape, 0) for s in state]
    vals = [[] for _ in items]
    pays = [[] for _ in items]
    for _ in range(P_TOPK):
        for k, (_, payload) in enumerate(items):
            s, row = state[k], rows[k]
            m = jnp.max(s, axis=0, keepdims=True)
            pos = jnp.min(jnp.where(s == m, row, s.shape[0]), axis=0, keepdims=True)
            hit = row == pos
            vals[k].append(m)
            pays[k].append(pos if payload is None else jnp.max(jnp.where(hit, payload, -1), axis=0, keepdims=True))
            state[k] = jnp.where(hit, -jnp.inf, s)
    return [(jnp.concatenate(v, axis=0), jnp.concatenate(p, axis=0)) for v, p in zip(vals, pays)]


def _retrieval(score_pairs):
    tops = _top16_rows([(s, None) for pair in score_pairs for s in pair])
    cands = []
    for u in range(len(score_pairs)):
        (s0, i0), (s1, i1) = tops[2 * u], tops[2 * u + 1]
        tn = s0.shape[1]
        cand = [s0[0:1, :] + s1]
        cidx = [i0[0:1, :] * N_KEYS + i1]
        b8 = _iota((8, tn), 0)
        for a in range(1, 8):
            va = s0[a:a + 1, :] + s1[0:8, :]
            nb = P_TOPK // (a + 1)
            cand.append(va if nb >= 8 else jnp.where(b8 < nb, va, -jnp.inf))
            cidx.append(i0[a:a + 1, :] * N_KEYS + i1[0:8, :])
        cand.append(s0[8:16, :] + s1[0:1, :])
        cidx.append(i0[8:16, :] * N_KEYS + i1[0:1, :])
        cands.append((jnp.concatenate(cand, axis=0), jnp.concatenate(cidx, axis=0)))
    out = []
    for best, e in _top16_rows(cands):
        w = jnp.exp(best - best[0:1, :])
        out.append((e, w / jnp.sum(w, axis=0, keepdims=True)))
    return out


def _peer_topk_kernel(q_ref, keys_ref, eidx_ref, gate_ref):
    eidx, gates = [], []
    for h0 in range(0, P_HEADS, 2):
        scores = [tuple(_dot_nt(keys_ref[h, half], q_ref[2 * h + half], HI) for half in range(2))
                  for h in (h0, h0 + 1)]
        for e, g in _retrieval(scores):
            eidx.append(e)
            gates.append(g)
    eidx_ref[...] = jnp.concatenate(eidx, axis=0).T
    gate_ref[...] = jnp.concatenate(gates, axis=0).T


def peer_topk(q, keys, tn):
    n = q.shape[1]
    hk = P_HEADS * P_TOPK
    return pl.pallas_call(
        _peer_topk_kernel, grid=(n // tn,),
        out_shape=[jax.ShapeDtypeStruct((n, hk), I32), jax.ShapeDtypeStruct((n, hk), F32)],
        in_specs=[pl.BlockSpec((N_SUBQ, tn, P_DK), lambda i: (0, i, 0)), pl.BlockSpec(keys.shape, lambda i: (0, 0, 0, 0))],
        out_specs=[pl.BlockSpec((tn, hk), lambda i: (i, 0))] * 2,
        compiler_params=_params(("parallel",)), name="peer_topk",
    )(q, keys)


E_CHUNK = 1024
SLABS = E_CHUNK // N_KEYS
KEY_BITS = N_KEYS.bit_length() - 1


N_CHUNKS = N_EXPERTS // E_CHUNK
N_PAIRS = N_CHUNKS // 2
HALF_KEYS = N_KEYS // 2
TOKEN_UNROLL = 8
U32 = jnp.uint32
HI_HALF = 0xFFFF0000


def _expert_scores(step, x_ref, ut_ref, eidx_ref, pf_ref):
    e = eidx_ref[...]
    ii = jnp.right_shift(e, KEY_BITS)
    jj = jnp.bitwise_and(e, N_KEYS - 1)
    pc = _dot(x_ref[...], ut_ref[...])
    pf = pf_ref[...]
    for s in range(SLABS):
        got = jnp.take_along_axis(pc[:, s * N_KEYS:(s + 1) * N_KEYS], jj, axis=1)
        pf = jnp.where(ii == step * SLABS + s, got, pf)
    pf_ref[...] = pf


def _expert_hidden(eidx_ref, gate_ref, pf_ref, hm_ref):
    tn, hk = pf_ref.shape
    pf_ref[...] = _gelu_erf(pf_ref[...]) * gate_ref[...]
    sub = _iota((N_KEYS, hk), 0)

    def tokens(t, carry):
        for r in range(TOKEN_UNROLL):
            n = t * TOKEN_UNROLL + r
            e = eidx_ref[pl.ds(n, 1), :]
            hrow = pf_ref[pl.ds(n, 1), :]
            w = jnp.where(sub == jnp.right_shift(e, KEY_BITS), hrow, 0.0).astype(BF16)
            bt = jnp.where(sub == jnp.bitwise_and(e, N_KEYS - 1), 1.0, 0.0).astype(BF16)
            hn = _dot_nt(w, bt).astype(BF16).astype(F32)
            hi = pltpu.bitcast(hn[:HALF_KEYS, :], U32)
            lo = lax.shift_right_logical(pltpu.bitcast(hn[HALF_KEYS:, :], U32), U32(16))
            hm_ref[pl.ds(pl.multiple_of(n * HALF_KEYS, HALF_KEYS), HALF_KEYS), :] = hi | lo
        return carry

    lax.fori_loop(0, tn // TOKEN_UNROLL, tokens, 0)


def _expert_values(cp, hm_ref, vlo_ref, vhi_ref, o_ref):
    tn = o_ref.shape[0]
    words = [hm_ref[pl.ds(cp * SLABS + s, tn, stride=HALF_KEYS), :] for s in range(SLABS)]
    hi = [pltpu.bitcast(wd & U32(HI_HALF), F32).astype(BF16) for wd in words]
    lo = [pltpu.bitcast(lax.shift_left(wd, U32(16)), F32).astype(BF16) for wd in words]
    o_ref[...] += (_dot(jnp.concatenate(hi, axis=1), vlo_ref[...])
                   + _dot(jnp.concatenate(lo, axis=1), vhi_ref[...]))


def _peer_expert_kernel(x_ref, h_ref, eidx_ref, gate_ref, ut_ref, vlo_ref, vhi_ref, o_ref, pf_ref, hm_ref):
    step = pl.program_id(1)

    @pl.when(step == 0)
    def _():
        pf_ref[...] = jnp.zeros(pf_ref.shape, F32)

    @pl.when(step < N_CHUNKS)
    def _():
        _expert_scores(step, x_ref, ut_ref, eidx_ref, pf_ref)

    @pl.when(step == N_CHUNKS - 1)
    def _():
        _expert_hidden(eidx_ref, gate_ref, pf_ref, hm_ref)
        o_ref[...] = h_ref[...]

    @pl.when(step >= N_CHUNKS)
    def _():
        _expert_values(step - N_CHUNKS, hm_ref, vlo_ref, vhi_ref, o_ref)


SUBTILE = 128


def _peer_fused_kernel(x_ref, h_ref, q_ref, keys_ref, ut_ref, vlo_ref, vhi_ref, o_ref,
                       pf_ref, hm_ref, eidx_ref, gate_ref, et_ref, gt_ref):
    tile = pl.program_id(0)
    step = pl.program_id(1)
    hk = P_HEADS * P_TOPK
    n_sub = et_ref.shape[1]
    slot = tile % 2

    @pl.when((tile == 0) & (step == 0))
    def _():
        et_ref[...] = jnp.zeros(et_ref.shape, I32)
        gt_ref[...] = jnp.zeros(gt_ref.shape, F32)

    @pl.when(step == 0)
    def _():
        for sub in range(n_sub):
            rows = slice(sub * SUBTILE, (sub + 1) * SUBTILE)
            eidx_ref[rows, :] = et_ref[1 - slot, sub].reshape(hk, SUBTILE).T
            gate_ref[rows, :] = gt_ref[1 - slot, sub].reshape(hk, SUBTILE).T
        pf_ref[...] = jnp.zeros(pf_ref.shape, F32)

    def retrieve(units):
        where, scores = [], []
        for u in units:
            head = u % P_HEADS
            sub = u // P_HEADS
            rows = pl.ds(pl.multiple_of(sub * SUBTILE, SUBTILE), SUBTILE)
            where.append((sub, head))
            scores.append(tuple(_dot_nt(keys_ref[head, half], q_ref[2 * head + half, rows, :], HI)
                                for half in range(2)))
        for (sub, head), (e, g) in zip(where, _retrieval(scores)):
            et_ref[slot, sub, head] = e
            gt_ref[slot, sub, head] = g

    @pl.when(step < N_CHUNKS)
    def _():
        retrieve([step])
        _expert_scores(step, x_ref, ut_ref, eidx_ref, pf_ref)

    @pl.when(step == N_CHUNKS - 1)
    def _():
        _expert_hidden(eidx_ref, gate_ref, pf_ref, hm_ref)
        o_ref[...] = h_ref[...]

    @pl.when(step >= N_CHUNKS)
    def _():
        cp = step - N_CHUNKS
        retrieve([N_CHUNKS + 2 * cp, N_CHUNKS + 2 * cp + 1])
        _expert_values(cp, hm_ref, vlo_ref, vhi_ref, o_ref)


def peer_fused(x2, h, q, keys, ut, v, tn):
    n = x2.shape[0]
    nt = n // tn
    hk = P_HEADS * P_TOPK
    n_sub = tn // SUBTILE
    assert n_sub * P_HEADS == N_CHUNKS + 2 * N_PAIRS
    prev = lambda w: pl.BlockSpec((tn, w), lambda i, s: (jnp.maximum(i - 1, 0), 0))
    pair = lambda s: jnp.clip(s - N_CHUNKS, 0, N_PAIRS - 1)
    return pl.pallas_call(
        _peer_fused_kernel, grid=(nt + 1, N_CHUNKS + N_PAIRS),
        out_shape=jax.ShapeDtypeStruct((n, D_MODEL), F32),
        in_specs=[prev(D_MODEL), prev(D_MODEL),
                  pl.BlockSpec((N_SUBQ, tn, P_DK), lambda i, s: (0, jnp.minimum(i, nt - 1), 0)),
                  pl.BlockSpec(keys.shape, lambda i, s: (0, 0, 0, 0)),
                  pl.BlockSpec((D_MODEL, E_CHUNK), lambda i, s: (0, jnp.minimum(s, N_CHUNKS - 1))),
                  pl.BlockSpec((E_CHUNK, D_MODEL), lambda i, s: (pair(s), 0)),
                  pl.BlockSpec((E_CHUNK, D_MODEL), lambda i, s: (pair(s) + N_PAIRS, 0))],
        out_specs=prev(D_MODEL),
        scratch_shapes=[pltpu.VMEM((tn, hk), F32), pltpu.VMEM((tn * HALF_KEYS, N_KEYS), U32),
                        pltpu.VMEM((tn, hk), I32), pltpu.VMEM((tn, hk), F32),
                        pltpu.VMEM((2, n_sub, P_HEADS, P_TOPK, SUBTILE), I32),
                        pltpu.VMEM((2, n_sub, P_HEADS, P_TOPK, SUBTILE), F32)],
        compiler_params=_params(("arbitrary", "arbitrary"), 58 * 1024 * 1024), name="peer_fused",
    )(x2, h, q, keys, ut, v, v)


def peer_experts(x2, h, eidx, gate, ut, v, tn):
    n = x2.shape[0]
    hk = P_HEADS * P_TOPK
    row = lambda w: pl.BlockSpec((tn, w), lambda i, s: (i, 0))
    pair = lambda s: jnp.clip(s - N_CHUNKS, 0, N_PAIRS - 1)
    return pl.pallas_call(
        _peer_expert_kernel, grid=(n // tn, N_CHUNKS + N_PAIRS),
        out_shape=jax.ShapeDtypeStruct((n, D_MODEL), F32),
        in_specs=[row(D_MODEL), row(D_MODEL), row(hk), row(hk),
                  pl.BlockSpec((D_MODEL, E_CHUNK), lambda i, s: (0, jnp.minimum(s, N_CHUNKS - 1))),
                  pl.BlockSpec((E_CHUNK, D_MODEL), lambda i, s: (pair(s), 0)),
                  pl.BlockSpec((E_CHUNK, D_MODEL), lambda i, s: (pair(s) + N_PAIRS, 0))],
        out_specs=row(D_MODEL),
        scratch_shapes=[pltpu.VMEM((tn, hk), F32), pltpu.VMEM((tn * HALF_KEYS, N_KEYS), U32)],
        compiler_params=_params(("parallel", "arbitrary"), 56 * 1024 * 1024), name="peer_experts",
    )(x2, h, eidx, gate, ut, v, v)


def _final_norm_kernel(h_ref, g_ref, o_ref):
    o_ref[...] = _rms(h_ref[...], g_ref[...])


def final_norm(h, g, tm):
    n = h.shape[0]
    row = pl.BlockSpec((tm, D_MODEL), lambda i: (i, 0))
    return pl.pallas_call(
        _final_norm_kernel, grid=(n // tm,), out_shape=jax.ShapeDtypeStruct(h.shape, F32),
        in_specs=[row, pl.BlockSpec(g.shape, lambda i: (0, 0))], out_specs=row,
        compiler_params=_params(("parallel",)), name="final_norm",
    )(h, g)


def _block_diag(w):
    g, a, b = w.shape
    eye = jnp.eye(g, dtype=w.dtype)
    return (eye[:, None, :, None] * w[:, :, None, :]).reshape(g * a, g * b)


def _prep_layer(W, l):
    w_in = W["w_in"][l]
    return dict(
        g1=W["norm1_g"][l][None, :],
        wa=w_in[:, OFF_AQ:OFF_AF].astype(BF16),
        wb=w_in[:, OFF_BX:OFF_CQKV].astype(BF16),
        wc=jnp.concatenate([w_in[:, OFF_CQKV:OFF_CA], w_in[:, OFF_CG:D_IN]], axis=1).astype(BF16),
        ws=jnp.concatenate([w_in[:, OFF_AF:OFF_BX], w_in[:, OFF_CA:OFF_CG]], axis=1).T,
        bf=W["b_f"][l][:, None],
        lru_cw=W["lru_conv_w"][l], lru_cb=W["lru_conv_b"][l][None, :],
        lru_wa=_block_diag(W["lru_wa"][l]), lru_ba=W["lru_ba"][l][None, :],
        lru_wx=_block_diag(W["lru_wx"][l]), lru_bx=W["lru_bx"][l][None, :],
        lru_lam=W["lru_lambda"][l][None, :],
        gdn_cw=W["gdn_conv_w"][l], gdn_alog=W["gdn_A_log"][l][:, None], gdn_dt=W["gdn_dt_bias"][l][:, None],
        gdn_ng=jnp.tile(W["gdn_norm_g"][l], C_HEADS)[None, :],
        wo=W["w_out"][l].astype(BF16), g2=W["norm2_g"][l][None, :], wq=W["peer_wq"][l].astype(BF16),
        keys=W["peer_keys"][l], ut=W["peer_u"][l].T.astype(BF16), v=W["peer_v"][l].astype(BF16),
    )


def _pair_state(s):
    nb = s.shape[0]
    s = s.reshape(nb, C_HEADS // 2, 2, C_HEAD_DIM, C_HEAD_DIM)
    eye = jnp.eye(2, dtype=s.dtype)
    return (s[:, :, :, :, None, :] * eye[None, None, :, None, :, None]).reshape(nb, C_HEADS // 2, PAIR, PAIR)


def _unpair_state(s):
    nb = s.shape[0]
    s = s.reshape(nb, C_HEADS // 2, 2, C_HEAD_DIM, 2, C_HEAD_DIM)
    return jnp.stack([s[:, :, 0, :, 0, :], s[:, :, 1, :, 1, :]], axis=2).reshape(nb, C_HEADS, C_HEAD_DIM, C_HEAD_DIM)


def _trunk(x, nb, t_real, paged, lru_h, lru_buf, gdn_s, gdn_buf, layers, final_g, cfg):
    t_rows = x.shape[1]
    n = nb * t_rows
    h = x.reshape(n, D_MODEL)
    ones_hd = _block_diag(jnp.full((C_HEADS, C_HEAD_DIM, C_HEAD_DIM), 1.0 / C_HEAD_DIM, F32))
    outs = []
    for l, lw in enumerate(layers):
        q, k, v, kb, vb, zb, zc, st = in_proj(h, lw["g1"], lw["wa"], lw["wb"], lw["wc"], lw["ws"], cfg["tm"])
        if paged is None:
            lf, c = logf_rows(st, lw["bf"], nb, True)
            ya = attn_prompt(q, kb, vb, c.reshape(A_HEADS // 2, 2, n), nb, cfg["tq"])
            n_chunks = cfg["tc_gdn"] // GDN_CHUNK
            scal = st.reshape(N_SMALL, n // GDN_CHUNK, GDN_CHUNK).transpose(1, 0, 2)
        else:
            ck, cv, clt, pt = paged
            lf, _ = logf_rows(st, lw["bf"], 1, False)
            lfn = lf.reshape(A_HEADS, nb, t_rows).transpose(1, 0, 2)
            lfn = jnp.where(jnp.arange(t_rows) < t_real, lfn, 0.0)
            lfn = jnp.pad(lfn, ((0, 0), (0, 0), (0, PAGE_SIZE - t_rows)))
            ya = attn_sample(pt, q, ck, cv, clt, k, v, lfn, l, t_real)
            n_chunks = 1
            scal = jnp.pad(st.reshape(N_SMALL, nb, t_rows).transpose(1, 0, 2),
                           ((0, 0), (0, 0), (0, GDN_CHUNK - t_rows)))
        t_valid = min(t_real, cfg["tc_lru"])
        yb, lru_ht, lru_nb = lru_mixer(zb, lru_buf[l], lru_h[l][:, None, :], lw["lru_cw"], lw["lru_cb"],
                                       lw["lru_wa"], lw["lru_ba"], lw["lru_wx"], lw["lru_bx"], lw["lru_lam"],
                                       nb, cfg["tc_lru"], t_valid)
        t_valid = min(t_real, cfg["tc_gdn"])
        yc, gdn_st, gdn_nb = gdn_mixer(zc, scal, gdn_buf[l], _pair_state(gdn_s[l]), lw["gdn_cw"], lw["gdn_alog"],
                                       lw["gdn_dt"], lw["gdn_ng"], ones_hd, nb, cfg["tc_gdn"], t_valid, n_chunks)
        hmid, x2, qp = out_proj(h, ya, yb, yc, lw["wo"], lw["g2"], lw["wq"], cfg["tm"])
        if cfg["fused_peer"]:
            h = peer_fused(x2, hmid, qp, lw["keys"], lw["ut"], lw["v"], cfg["tn_peer"])
        else:
            eidx, gate = peer_topk(qp, lw["keys"], cfg["tn_topk"])
            h = peer_experts(x2, hmid, eidx, gate, lw["ut"], lw["v"], cfg["tn_peer"])
        outs.append((k.reshape(nb, t_rows, A_HEADS, A_HEAD_DIM)[:, :t_real],
                     v.reshape(nb, t_rows, A_HEADS, A_HEAD_DIM)[:, :t_real],
                     lf.reshape(A_HEADS, nb, t_rows).transpose(1, 2, 0)[:, :t_real],
                     lru_ht[:, 0, :], lru_nb, _unpair_state(gdn_st), gdn_nb))
    y = final_norm(h, final_g[None, :], cfg["tm"]).reshape(nb, t_rows, D_MODEL)[:, :t_real]
    return y, tuple(jnp.stack([o[i] for o in outs], axis=0) for i in range(len(outs[0])))


PROMPT_CFG = dict(tm=512, tq=1024, tc_lru=512, tc_gdn=256, tn_topk=None, tn_peer=512, fused_peer=True)
SAMPLE_CFG = dict(tm=256, tq=None, tc_lru=SAMPLE_ROWS, tc_gdn=SAMPLE_ROWS, tn_topk=128, tn_peer=256, fused_peer=False)


def kernel(x_prompt, x_sample, cache_k, cache_v, cache_logf, page_table, state_lru_h, state_lru_conv,
           state_gdn_S, state_gdn_conv, norm1_g, w_in, b_f, lru_conv_w, lru_conv_b, lru_wa, lru_ba,
           lru_wx, lru_bx, lru_lambda, gdn_conv_w, gdn_A_log, gdn_dt_bias, gdn_norm_g, w_out, norm2_g,
           peer_wq, peer_keys, peer_u, peer_v, final_norm_g):
    W = dict(norm1_g=norm1_g, w_in=w_in, b_f=b_f, lru_conv_w=lru_conv_w, lru_conv_b=lru_conv_b, lru_wa=lru_wa,
             lru_ba=lru_ba, lru_wx=lru_wx, lru_bx=lru_bx, lru_lambda=lru_lambda, gdn_conv_w=gdn_conv_w,
             gdn_A_log=gdn_A_log, gdn_dt_bias=gdn_dt_bias, gdn_norm_g=gdn_norm_g, w_out=w_out, norm2_g=norm2_g,
             peer_wq=peer_wq, peer_keys=peer_keys, peer_u=peer_u, peer_v=peer_v)
    depth = w_in.shape[0]
    layers = [_prep_layer(W, l) for l in range(depth)]
    bp, tp = x_prompt.shape[:2]
    bs, ts = x_sample.shape[:2]

    ck = cache_k.transpose(0, 1, 3, 4, 2)
    cv = cache_v.transpose(0, 1, 3, 4, 2)
    clt = cache_logf.transpose(0, 1, 3, 2)
    xs = jnp.pad(x_sample, ((0, 0), (0, SAMPLE_ROWS - ts), (0, 0)))
    ys, ss = _trunk(xs, bs, ts, (ck, cv, clt, page_table), state_lru_h, state_lru_conv, state_gdn_S,
                    state_gdn_conv, layers, final_norm_g, SAMPLE_CFG)

    zeros = lambda *s: jnp.zeros((depth, bp) + s, F32)
    yp, sp = _trunk(x_prompt, bp, tp, None, zeros(B_WIDTH), zeros(CONV_W - 1, B_WIDTH),
                    zeros(C_HEADS, C_HEAD_DIM, C_HEAD_DIM), zeros(CONV_W - 1, 3 * C_WIDTH),
                    layers, final_norm_g, PROMPT_CFG)

    return (yp, ys, sp[0], sp[1], sp[2], ss[0], ss[1], ss[2], sp[3], ss[3], sp[4], ss[4], sp[5], ss[5], sp[6], ss[6])
```

```python
import functools

import jax
import jax.numpy as jnp
from jax import lax
from jax.experimental import pallas as pl
from jax.experimental.pallas import tpu as pltpu

F32 = jnp.float32
BF16 = jnp.bfloat16
I32 = jnp.int32
HI = lax.Precision.HIGHEST

D_MODEL = 1024
A_HEADS = 8
A_HEAD_DIM = 64
A_WIDTH = A_HEADS * A_HEAD_DIM
B_WIDTH = 256
B_BLOCKS = 4
LRU_C = 8.0
C_HEADS = 4
C_HEAD_DIM = 64
C_WIDTH = C_HEADS * C_HEAD_DIM
GDN_CHUNK = 64
CONV_W = 4
P_HEADS = 8
P_DK = 128
N_KEYS = 128
N_EXPERTS = N_KEYS * N_KEYS
P_TOPK = 16
PAGE_SIZE = 128
EPS = 1e-6
SAMPLE_ROWS = 8
N_SMALL = 16

OFF_AQ = 0
OFF_AF = OFF_AQ + 3 * A_WIDTH
OFF_BX = OFF_AF + A_HEADS
OFF_CQKV = OFF_BX + 2 * B_WIDTH
OFF_CA = OFF_CQKV + 3 * C_WIDTH
OFF_CG = OFF_CA + 2 * C_HEADS
D_IN = OFF_CG + C_WIDTH

VMEM_LIMIT = 48 * 1024 * 1024


def _params(sem, vmem=VMEM_LIMIT):
    return pltpu.CompilerParams(dimension_semantics=sem, vmem_limit_bytes=vmem)


def _dot(a, b, precision=None):
    return jnp.dot(a, b, precision=precision, preferred_element_type=F32)


def _dot_nt(a, b, precision=None):
    return lax.dot_general(a, b, (((1,), (1,)), ((), ())), precision=precision, preferred_element_type=F32)


def _dot_tn(a, b, precision=None):
    return lax.dot_general(a, b, (((0,), (0,)), ((), ())), precision=precision, preferred_element_type=F32)


_DOTS = {"nn": _dot, "nt": _dot_nt, "tn": _dot_tn}


def _split_bf16(a):
    hi = a.astype(BF16)
    return hi, (a - hi.astype(F32)).astype(BF16)


def _mm(a, b, prec, kind="nn"):
    dot = _DOTS[kind]
    if prec == "bf":
        return dot(a.astype(BF16), b.astype(BF16))
    if prec == "x2l":
        ah, al = _split_bf16(a)
        bb = b.astype(BF16)
        return dot(ah, bb) + dot(al, bb)
    if prec == "x3":
        ah, al = _split_bf16(a)
        bh, bl = _split_bf16(b)
        return dot(ah, bh) + (dot(ah, bl) + dot(al, bh))
    return dot(a, b, HI)


def _rms(x, g):
    return x * lax.rsqrt(jnp.mean(x * x, axis=-1, keepdims=True) + EPS) * g


def _gelu_erf(x):
    return 0.5 * x * (1.0 + lax.erf(x * 0.7071067811865476))


def _iota(shape, dim):
    return lax.broadcasted_iota(I32, shape, dim)


def _inproj_kernel(x_ref, g_ref, wa_ref, wb_ref, wc_ref, ws_ref,
                   q_ref, k_ref, v_ref, kb_ref, vb_ref, zb_ref, zc_ref, st_ref):
    hn = _rms(x_ref[...], g_ref[...])
    hb = hn.astype(BF16)
    za = _dot(hb, wa_ref[...])
    q_ref[...] = (za[:, :A_WIDTH] * (A_HEAD_DIM ** -0.5)).astype(BF16)
    k = za[:, A_WIDTH:2 * A_WIDTH]
    v = za[:, 2 * A_WIDTH:]
    k_ref[...] = k
    v_ref[...] = v
    kb_ref[...] = k.astype(BF16)
    vb_ref[...] = v.astype(BF16)
    zb_ref[...] = _dot(hb, wb_ref[...])
    zc_ref[...] = _dot(hb, wc_ref[...])
    st_ref[...] = _dot_nt(ws_ref[...], hn, HI)


def in_proj(h, g, wa, wb, wc, ws, tm):
    n = h.shape[0]
    grid = (n // tm,)
    row = lambda w: pl.BlockSpec((tm, w), lambda i: (i, 0))
    full = lambda a: pl.BlockSpec(a.shape, lambda i: (0,) * a.ndim)
    out_shape = [
        jax.ShapeDtypeStruct((n, A_WIDTH), BF16),
        jax.ShapeDtypeStruct((n, A_WIDTH), F32),
        jax.ShapeDtypeStruct((n, A_WIDTH), F32),
        jax.ShapeDtypeStruct((n, A_WIDTH), BF16),
        jax.ShapeDtypeStruct((n, A_WIDTH), BF16),
        jax.ShapeDtypeStruct((n, 2 * B_WIDTH), F32),
        jax.ShapeDtypeStruct((n, 4 * C_WIDTH), F32),
        jax.ShapeDtypeStruct((N_SMALL, n), F32),
    ]
    out_specs = [row(A_WIDTH)] * 5 + [row(2 * B_WIDTH), row(4 * C_WIDTH),
                                      pl.BlockSpec((N_SMALL, tm), lambda i: (0, i))]
    return pl.pallas_call(
        _inproj_kernel, grid=grid, out_shape=out_shape,
        in_specs=[row(D_MODEL), full(g), full(wa), full(wb), full(wc), full(ws)],
        out_specs=out_specs, compiler_params=_params(("parallel",)), name="in_proj",
    )(h, g, wa, wb, wc, ws)


CUMSUM_CHUNK = 512


def _logf_kernel(st_ref, bf_ref, lf_ref, c_ref, *, cumsum):
    lf = jax.nn.log_sigmoid(st_ref[0:A_HEADS, :] + bf_ref[...])
    lf_ref[...] = lf
    if not cumsum:
        c_ref[...] = lf
        return
    t = lf.shape[1]
    ch = min(CUMSUM_CHUNK, t)
    tri = (_iota((ch, ch), 0) <= _iota((ch, ch), 1)).astype(F32)
    carry = jnp.zeros((A_HEADS, 1), F32)
    for j in range(t // ch):
        cs = _dot(lf[:, j * ch:(j + 1) * ch], tri, HI) + carry
        c_ref[:, j * ch:(j + 1) * ch] = cs
        carry = cs[:, ch - 1:ch]


def logf_rows(st, bf, nb, cumsum):
    n = st.shape[1]
    t = n // nb
    spec_in = pl.BlockSpec((N_SMALL, t), lambda b: (0, b))
    spec_out = pl.BlockSpec((A_HEADS, t), lambda b: (0, b))
    return pl.pallas_call(
        functools.partial(_logf_kernel, cumsum=cumsum), grid=(nb,),
        out_shape=[jax.ShapeDtypeStruct((A_HEADS, n), F32)] * 2,
        in_specs=[spec_in, pl.BlockSpec(bf.shape, lambda b: (0, 0))],
        out_specs=[spec_out, spec_out], compiler_params=_params(("parallel",)), name="logf",
    )(st, bf)


def _attn_prompt_kernel(qi_ref, ki_ref, q_ref, k_ref, v_ref, c_ref, o_ref, m_ref, l_ref, acc_ref):
    step = pl.program_id(2)
    qi = qi_ref[step]
    ki = ki_ref[step]
    tq, tk = q_ref.shape[0], k_ref.shape[0]
    w = 2 * A_HEAD_DIM
    lane = _iota((1, w), 1)

    @pl.when(ki == 0)
    def _():
        m_ref[...] = jnp.full(m_ref.shape, -jnp.inf, F32)
        l_ref[...] = jnp.zeros(l_ref.shape, F32)
        acc_ref[...] = jnp.zeros(acc_ref.shape, F32)

    def block(masked):
        q = q_ref[...]
        k = k_ref[...]
        v = v_ref[...]
        for h in range(2):
            head = (lane >= A_HEAD_DIM) if h else (lane < A_HEAD_DIM)
            s = _dot_nt(jnp.where(head, q, jnp.zeros_like(q)), k) - c_ref[0, h:h + 1, :]
            if masked:
                s = jnp.where(_iota((tq, tk), 1) <= _iota((tq, tk), 0), s, -jnp.inf)
            m_prev = m_ref[h]
            m_new = jnp.maximum(m_prev, jnp.max(s, axis=1, keepdims=True))
            alpha = jnp.exp(m_prev - m_new)
            p = jnp.exp(s - jnp.concatenate([m_new] * (tk // w), axis=1))
            l_ref[h] = alpha * l_ref[h] + jnp.sum(p, axis=1, keepdims=True)
            acc_ref[h] = alpha * acc_ref[h] + _dot(p.astype(BF16), v)
            m_ref[h] = m_new

    @pl.when(ki < qi)
    def _():
        block(False)

    @pl.when(ki == qi)
    def _():
        block(True)
        o0 = acc_ref[0] / l_ref[0]
        o1 = acc_ref[1] / l_ref[1]
        o_ref[...] = jnp.where(lane < A_HEAD_DIM, o0, o1).astype(o_ref.dtype)


def attn_prompt(q, kb, vb, c, nb, tq):
    n = q.shape[0]
    t = n // nb
    nq = t // tq
    w = 2 * A_HEAD_DIM
    pairs = [(i, j) for i in range(nq) for j in range(i + 1)]
    qi_list = jnp.asarray([i for i, _ in pairs], I32)
    ki_list = jnp.asarray([j for _, j in pairs], I32)
    qspec = pl.BlockSpec((tq, w), lambda b, p, s, qi, ki: (b * nq + qi[s], p))
    kspec = pl.BlockSpec((tq, w), lambda b, p, s, qi, ki: (b * nq + ki[s], p))
    cspec = pl.BlockSpec((1, 2, tq), lambda b, p, s, qi, ki: (p, 0, b * nq + ki[s]))
    grid_spec = pltpu.PrefetchScalarGridSpec(
        num_scalar_prefetch=2, grid=(nb, A_HEADS // 2, len(pairs)),
        in_specs=[qspec, kspec, kspec, cspec], out_specs=qspec,
        scratch_shapes=[pltpu.VMEM((2, tq, w), F32), pltpu.VMEM((2, tq, w), F32), pltpu.VMEM((2, tq, w), F32)])
    return pl.pallas_call(
        _attn_prompt_kernel, grid_spec=grid_spec,
        out_shape=jax.ShapeDtypeStruct((n, A_WIDTH), BF16),
        compiler_params=_params(("parallel", "parallel", "arbitrary")), name="attn_prompt",
    )(qi_list, ki_list, q, kb, vb, c)


PAGE_GROUP = 8


def _attn_sample_kernel(pt_ref, q_ref, kn_ref, vn_ref, ln_ref, ck_hbm, cv_hbm, cl_hbm, o_ref,
                        kbuf, vbuf, lbuf, c_ref, qh_ref, m_ref, l_ref, acc_ref, ksem, vsem, lsem,
                        *, layer, n_pages, t_new):
    b = pl.program_id(0)
    nr = A_HEADS * SAMPLE_ROWS
    hd = A_HEAD_DIM
    n_groups = n_pages // PAGE_GROUP

    def kv_copies(g, slot):
        cps = []
        for i in range(PAGE_GROUP):
            pg = pt_ref[b, g * PAGE_GROUP + i]
            cps.append(pltpu.make_async_copy(ck_hbm.at[layer, pg], kbuf.at[slot, i], ksem.at[slot]))
            cps.append(pltpu.make_async_copy(cv_hbm.at[layer, pg], vbuf.at[slot, i], vsem.at[slot]))
        return cps

    def lf_copy(j):
        return pltpu.make_async_copy(cl_hbm.at[layer, pt_ref[b, j]], lbuf.at[j], lsem)

    for j in range(n_pages):
        lf_copy(j).start()
    for cp in kv_copies(0, 0):
        cp.start()

    q = q_ref[...].astype(F32)
    for h in range(A_HEADS):
        qh_ref[h] = q[:, h * hd:(h + 1) * hd]
    m_ref[...] = jnp.full(m_ref.shape, -jnp.inf, F32)
    l_ref[...] = jnp.zeros(l_ref.shape, F32)
    acc_ref[...] = jnp.zeros(acc_ref.shape, F32)

    for j in range(n_pages):
        lf_copy(j).wait()
    tri = (_iota((PAGE_SIZE, PAGE_SIZE), 0) <= _iota((PAGE_SIZE, PAGE_SIZE), 1)).astype(F32)
    nrow = n_pages * A_HEADS
    cs = _dot(lbuf[...].reshape(nrow, PAGE_SIZE), tri, HI)
    totals = jnp.broadcast_to(cs[:, PAGE_SIZE - 1:PAGE_SIZE], (nrow, PAGE_SIZE))
    r, cc = _iota((nrow, nrow), 0), _iota((nrow, nrow), 1)
    earlier = ((cc // A_HEADS < r // A_HEADS) & (cc % A_HEADS == r % A_HEADS)).astype(F32)
    c_all = cs + _dot(earlier, totals, HI)
    c_ref[...] = c_all.reshape(n_pages, A_HEADS, PAGE_SIZE)
    carry = c_all[nrow - A_HEADS:, PAGE_SIZE - 1:PAGE_SIZE]

    def update(s, weighted):
        m_prev = m_ref[...]
        m_new = jnp.maximum(m_prev, jnp.max(s, axis=1, keepdims=True))
        alpha = jnp.exp(m_prev - m_new)
        p = jnp.exp(s - m_new)
        l_ref[...] = alpha * l_ref[...] + jnp.sum(p, axis=1, keepdims=True)
        pv = jnp.concatenate([weighted(h, p[h * SAMPLE_ROWS:(h + 1) * SAMPLE_ROWS, :].astype(BF16))
                              for h in range(A_HEADS)], axis=0)
        acc_ref[...] = alpha * acc_ref[...] + pv
        m_ref[...] = m_new

    def group(g, carry_):
        slot = g % 2

        @pl.when(g + 1 < n_groups)
        def _():
            for cp in kv_copies(g + 1, 1 - slot):
                cp.start()

        for cp in kv_copies(g, slot):
            cp.wait()
        kb = kbuf.at[slot]
        vb = vbuf.at[slot]
        tiles = []
        for i in range(PAGE_GROUP):
            c = c_ref[g * PAGE_GROUP + i]
            tiles.append(jnp.concatenate(
                [_dot(qh_ref[h].astype(BF16), kb[i, h].astype(BF16)) - c[h:h + 1, :] for h in range(A_HEADS)], axis=0))

        def weighted(h, ph):
            out = _dot_nt(ph[:, 0:PAGE_SIZE], vb[0, h].astype(BF16))
            for i in range(1, PAGE_GROUP):
                out = out + _dot_nt(ph[:, i * PAGE_SIZE:(i + 1) * PAGE_SIZE], vb[i, h].astype(BF16))
            return out

        update(jnp.concatenate(tiles, axis=1), weighted)
        return carry_

    lax.fori_loop(0, n_groups, group, 0)

    pad = jnp.zeros((PAGE_SIZE - SAMPLE_ROWS, hd), F32)
    rows = lambda ref, h: jnp.concatenate([ref[:, h * hd:(h + 1) * hd], pad], axis=0).astype(BF16)
    c_new = _dot(ln_ref[...], tri, HI) + carry
    visible = _iota((nr, PAGE_SIZE), 1) <= _iota((nr, PAGE_SIZE), 0) % SAMPLE_ROWS
    s_new = jnp.concatenate([_dot_nt(qh_ref[h].astype(BF16), rows(kn_ref, h)) - c_new[h:h + 1, :]
                             for h in range(A_HEADS)], axis=0)
    update(jnp.where(visible, s_new, -jnp.inf), lambda h, ph: _dot(ph, rows(vn_ref, h)))

    o = acc_ref[...] / l_ref[...]
    o = jnp.concatenate([o[h * SAMPLE_ROWS:(h + 1) * SAMPLE_ROWS, :] for h in range(A_HEADS)], axis=1)
    o_ref[...] = jnp.where(_iota(o.shape, 0) < t_new, o, 0.0).astype(o_ref.dtype)


def attn_sample(page_table, q, cache_k, cache_v, cache_lt, kn, vn, lfn, layer, t_new):
    nb, n_pages = page_table.shape
    assert n_pages % PAGE_GROUP == 0
    nr = A_HEADS * SAMPLE_ROWS
    seq = lambda w: pl.BlockSpec((SAMPLE_ROWS, w), lambda b, pt: (b, 0))
    hbm = pl.BlockSpec(memory_space=pl.ANY)
    page_buf = pltpu.VMEM((2, PAGE_GROUP, A_HEADS, A_HEAD_DIM, PAGE_SIZE), F32)
    grid_spec = pltpu.PrefetchScalarGridSpec(
        num_scalar_prefetch=1, grid=(nb,),
        in_specs=[seq(A_WIDTH), seq(A_WIDTH), seq(A_WIDTH),
                  pl.BlockSpec((None, A_HEADS, PAGE_SIZE), lambda b, pt: (b, 0, 0)), hbm, hbm, hbm],
        out_specs=seq(A_WIDTH),
        scratch_shapes=[page_buf, page_buf,
                        pltpu.VMEM((n_pages, A_HEADS, PAGE_SIZE), F32), pltpu.VMEM((n_pages, A_HEADS, PAGE_SIZE), F32),
                        pltpu.VMEM((A_HEADS, SAMPLE_ROWS, A_HEAD_DIM), F32), pltpu.VMEM((nr, 1), F32),
                        pltpu.VMEM((nr, 1), F32), pltpu.VMEM((nr, A_HEAD_DIM), F32),
                        pltpu.SemaphoreType.DMA((2,)), pltpu.SemaphoreType.DMA((2,)), pltpu.SemaphoreType.DMA(())])
    return pl.pallas_call(
        functools.partial(_attn_sample_kernel, layer=layer, n_pages=n_pages, t_new=t_new), grid_spec=grid_spec,
        out_shape=jax.ShapeDtypeStruct(q.shape, BF16),
        compiler_params=_params(("arbitrary",)), name="attn_sample",
    )(page_table, q, kn, vn, lfn, cache_k, cache_v, cache_lt)


CONV_PAD = 8


def _conv_block(xp_ref, x, w_ref, tc):
    xp_ref[CONV_PAD:CONV_PAD + tc, :] = x
    y = xp_ref[CONV_PAD - 3:CONV_PAD - 3 + tc, :] * w_ref[0:1, :]
    for i in range(1, CONV_W):
        y = y + xp_ref[CONV_PAD - 3 + i:CONV_PAD - 3 + i + tc, :] * w_ref[i:i + 1, :]
    return y


def _lru_kernel(z_ref, buf_ref, h0_ref, cw_ref, cb_ref, wa_ref, ba_ref, wx_ref, bx_ref, lam_ref,
                y_ref, ht_ref, nb_ref, xp_ref, hc_ref, *, t_valid):
    ti = pl.program_id(1)
    tc = z_ref.shape[0]

    @pl.when(ti == 0)
    def _():
        xp_ref[CONV_PAD - 3:CONV_PAD, :] = buf_ref[...]
        hc_ref[...] = h0_ref[...]

    x = z_ref[:, :B_WIDTH]
    xc = _conv_block(xp_ref, x, cw_ref, tc) + cb_ref[...]
    r = jax.nn.sigmoid(_dot(xc, wa_ref[...], HI) + ba_ref[...])
    ig = jax.nn.sigmoid(_dot(xc, wx_ref[...], HI) + bx_ref[...])
    log_a = -LRU_C * r * jax.nn.softplus(-lam_ref[...])
    a = jnp.exp(log_a)
    b = jnp.sqrt(1.0 - jnp.exp(2.0 * log_a)) * (ig * xc)
    row = _iota((tc, B_WIDTH), 0)
    d = 1
    while d < tc:
        keep = row >= d
        a_prev = jnp.where(keep, pltpu.roll(a, d, 0), 1.0)
        b_prev = jnp.where(keep, pltpu.roll(b, d, 0), 0.0)
        b = a * b_prev + b
        a = a * a_prev
        d *= 2
    hs = a * hc_ref[...] + b
    y_ref[...] = (hs * jax.nn.gelu(z_ref[:, B_WIDTH:])).astype(y_ref.dtype)
    hc_ref[...] = hs[t_valid - 1:t_valid, :]
    xp_ref[CONV_PAD - 3:CONV_PAD, :] = xp_ref[CONV_PAD + t_valid - 3:CONV_PAD + t_valid, :]

    @pl.when(ti == pl.num_programs(1) - 1)
    def _():
        ht_ref[...] = hc_ref[...]
        nb_ref[...] = xp_ref[CONV_PAD - 3:CONV_PAD, :]


def lru_mixer(zb, buf0, h0, cw, cb, wa, ba, wx, bx, lam, nb, tc, t_valid):
    n = zb.shape[0]
    nt = n // nb // tc
    full = lambda a: pl.BlockSpec(a.shape, lambda b, t: (0,) * a.ndim)
    per_b = lambda r: pl.BlockSpec((None, r, B_WIDTH), lambda b, t: (b, 0, 0))
    return pl.pallas_call(
        functools.partial(_lru_kernel, t_valid=t_valid), grid=(nb, nt),
        out_shape=[jax.ShapeDtypeStruct((n, B_WIDTH), BF16), jax.ShapeDtypeStruct((nb, 1, B_WIDTH), F32),
                   jax.ShapeDtypeStruct((nb, CONV_W - 1, B_WIDTH), F32)],
        in_specs=[pl.BlockSpec((tc, 2 * B_WIDTH), lambda b, t: (b * nt + t, 0)), per_b(CONV_W - 1), per_b(1),
                  full(cw), full(cb), full(wa), full(ba), full(wx), full(bx), full(lam)],
        out_specs=[pl.BlockSpec((tc, B_WIDTH), lambda b, t: (b * nt + t, 0)), per_b(1), per_b(CONV_W - 1)],
        scratch_shapes=[pltpu.VMEM((CONV_PAD + tc, B_WIDTH), F32), pltpu.VMEM((1, B_WIDTH), F32)],
        compiler_params=_params(("parallel", "arbitrary")), name="lru",
    )(zb, buf0, h0, cw, cb, wa, ba, wx, bx, lam)


PAIR = 2 * C_HEAD_DIM
GDN_INV_PREC = "bf"
GDN_PREC = "bf"
GDN_LOCKSTEP = 8


def _stack_heads(x, lo_mask):
    return jnp.concatenate([jnp.where(lo_mask, x, 0.0), jnp.where(lo_mask, 0.0, x)], axis=0)


def _row_to_col(row, eye):
    return jnp.sum(jnp.where(eye, row, 0.0), axis=1, keepdims=True)


def _gdn_kernel(z_ref, sc_ref, buf_ref, s0_ref, cw_ref, alog_ref, dt_ref, ng_ref, ones_hd_ref,
                y_ref, st_ref, nb_ref, xp_ref, s_ref, o_ref, ys_ref, wy_ref, eg_ref, *, t_valid, n_chunks):
    ti = pl.program_id(1)
    tc = z_ref.shape[0]
    c = GDN_CHUNK
    qkv_w = 3 * C_WIDTH

    @pl.when(ti == 0)
    def _():
        xp_ref[CONV_PAD - 3:CONV_PAD, :] = buf_ref[...]
        s_ref[...] = s0_ref[...]

    y = jax.nn.silu(_conv_block(xp_ref, z_ref[:, :qkv_w], cw_ref, tc))
    xp_ref[CONV_PAD - 3:CONV_PAD, :] = xp_ref[CONV_PAD + t_valid - 3:CONV_PAD + t_valid, :]
    padded = tc < n_chunks * c
    if padded:
        y = jnp.concatenate([y, jnp.zeros((n_chunks * c - tc, qkv_w), F32)], axis=0)
    ones_hd = ones_hd_ref[...]

    def l2n(x):
        return x * lax.rsqrt(_mm(x * x, ones_hd, "x2l") * C_HEAD_DIM + EPS)

    ys_ref[:, :C_WIDTH] = l2n(y[:, :C_WIDTH]) * (C_HEAD_DIM ** -0.5)
    ys_ref[:, C_WIDTH:2 * C_WIDTH] = l2n(y[:, C_WIDTH:2 * C_WIDTH])
    ys_ref[:, 2 * C_WIDTH:] = y[:, 2 * C_WIDTH:]

    r128 = _iota((PAIR, PAIR), 0)
    c128 = _iota((PAIR, PAIR), 1)
    same = (r128 // c) == (c128 // c)
    eye = r128 == c128
    incl = same & (c128 <= r128)
    strict = same & (c128 < r128)
    tri_bd = (same & (r128 <= c128)).astype(F32)
    last_sel = same & (c128 % c == c - 1)
    lo_lane = _iota((1, PAIR), 1) < c
    lane_t = _iota((1, PAIR), 1) % c
    ident = eye.astype(F32)

    def prepare(ci, p):
        sc = sc_ref[ci]
        g4 = -jnp.exp(alog_ref[...]) * jax.nn.softplus(sc[8:12, :] + dt_ref[...])
        b4 = jax.nn.sigmoid(sc[12:16, :])
        g_row = jnp.concatenate([g4[2 * p:2 * p + 1, :], g4[2 * p + 1:2 * p + 2, :]], axis=1)
        b_row = jnp.concatenate([b4[2 * p:2 * p + 1, :], b4[2 * p + 1:2 * p + 2, :]], axis=1)
        if padded:
            ok = lane_t < t_valid
            g_row = jnp.where(ok, g_row, 0.0)
            b_row = jnp.where(ok, b_row, 0.0)
        gc_row = _dot(jnp.broadcast_to(g_row, (8, PAIR)), tri_bd, HI)[0:1, :]
        gc_col = _row_to_col(gc_row, eye)
        b_col = _row_to_col(b_row, eye)
        gl_col = jnp.sum(jnp.where(last_sel, gc_row, 0.0), axis=1, keepdims=True)
        decay = jnp.where(incl, jnp.exp(jnp.where(incl, gc_col - gc_row, 0.0)), 0.0)
        rows = slice(ci * c, (ci + 1) * c)
        q_st = _stack_heads(ys_ref[rows, p * PAIR:(p + 1) * PAIR], lo_lane)
        k_st = _stack_heads(ys_ref[rows, C_WIDTH + p * PAIR:C_WIDTH + (p + 1) * PAIR], lo_lane)
        v_st = _stack_heads(ys_ref[rows, 2 * C_WIDTH + p * PAIR:2 * C_WIDTH + (p + 1) * PAIR], lo_lane)
        kb_st = k_st * b_col
        wy_ref[3, ci, p] = q_st * jnp.exp(gc_col)
        wy_ref[4, ci, p] = k_st * jnp.exp(gl_col - gc_col)
        eg_ref[ci, p] = jnp.exp(gl_col)
        return dict(decay=decay, q=q_st, k=k_st, kb=kb_st, vb=v_st * b_col, kbe=kb_st * jnp.exp(gc_col))

    chains = [(ci, p) for ci in range(n_chunks) for p in range(C_HEADS // 2)]
    for g0 in range(0, len(chains), GDN_LOCKSTEP):
        group = chains[g0:g0 + GDN_LOCKSTEP]
        st = [prepare(ci, p) for ci, p in group]
        lmat = [jnp.where(strict, _mm(d["kb"], d["k"], GDN_PREC, "nt") * d["decay"], 0.0) for d in st]
        tinv = [ident - m for m in lmat]
        pw = lmat
        for _ in range(5):
            pw = [_mm(x, x, GDN_INV_PREC) for x in pw]
            tinv = [t + _mm(t, x, GDN_INV_PREC) for t, x in zip(tinv, pw)]
        for (ci, p), d, t in zip(group, st, tinv):
            wy_ref[0, ci, p] = _mm(t, d["vb"], GDN_INV_PREC)
            wy_ref[1, ci, p] = _mm(t, d["kbe"], GDN_INV_PREC)
            wy_ref[2, ci, p] = _mm(d["q"], d["k"], GDN_PREC, "nt") * d["decay"]

    for ci in range(n_chunks):
        for p in range(C_HEADS // 2):
            s = s_ref[p]
            v_new = wy_ref[0, ci, p] - _mm(wy_ref[1, ci, p], s, GDN_PREC)
            o_st = _mm(wy_ref[3, ci, p], s, GDN_PREC) + _mm(wy_ref[2, ci, p], v_new, GDN_PREC)
            o_ref[ci * c:(ci + 1) * c, p * PAIR:(p + 1) * PAIR] = o_st[:c, :] + o_st[c:, :]
            s_ref[p] = s * eg_ref[ci, p] + _mm(wy_ref[4, ci, p], v_new, GDN_PREC, "tn")

    o = o_ref[0:tc, :]
    ms = _mm(o * o, ones_hd, "x2l")
    on = o * lax.rsqrt(ms + EPS) * ng_ref[...]
    y_ref[...] = (on * jax.nn.silu(z_ref[:, qkv_w:])).astype(y_ref.dtype)

    @pl.when(ti == pl.num_programs(1) - 1)
    def _():
        st_ref[...] = s_ref[...]
        nb_ref[...] = xp_ref[CONV_PAD - 3:CONV_PAD, :]


def gdn_mixer(zc, scal, buf0, s0, cw, alog, dt, ng, ones_hd, nb, tc, t_valid, n_chunks):
    n = zc.shape[0]
    nt = n // nb // tc
    qkv_w = 3 * C_WIDTH
    full = lambda a: pl.BlockSpec(a.shape, lambda b, t: (0,) * a.ndim)
    return pl.pallas_call(
        functools.partial(_gdn_kernel, t_valid=t_valid, n_chunks=n_chunks), grid=(nb, nt),
        out_shape=[jax.ShapeDtypeStruct((n, C_WIDTH), BF16),
                   jax.ShapeDtypeStruct((nb, C_HEADS // 2, PAIR, PAIR), F32),
                   jax.ShapeDtypeStruct((nb, CONV_W - 1, qkv_w), F32)],
        in_specs=[pl.BlockSpec((tc, 4 * C_WIDTH), lambda b, t: (b * nt + t, 0)),
                  pl.BlockSpec((n_chunks, N_SMALL, GDN_CHUNK), lambda b, t: (b * nt + t, 0, 0)),
                  pl.BlockSpec((None, CONV_W - 1, qkv_w), lambda b, t: (b, 0, 0)),
                  pl.BlockSpec((None, C_HEADS // 2, PAIR, PAIR), lambda b, t: (b, 0, 0, 0)),
                  full(cw), full(alog), full(dt), full(ng), full(ones_hd)],
        out_specs=[pl.BlockSpec((tc, C_WIDTH), lambda b, t: (b * nt + t, 0)),
                   pl.BlockSpec((None, C_HEADS // 2, PAIR, PAIR), lambda b, t: (b, 0, 0, 0)),
                   pl.BlockSpec((None, CONV_W - 1, qkv_w), lambda b, t: (b, 0, 0))],
        scratch_shapes=[pltpu.VMEM((CONV_PAD + tc, qkv_w), F32),
                        pltpu.VMEM((C_HEADS // 2, PAIR, PAIR), F32),
                        pltpu.VMEM((n_chunks * GDN_CHUNK, C_WIDTH), F32),
                        pltpu.VMEM((n_chunks * GDN_CHUNK, qkv_w), F32),
                        pltpu.VMEM((5, n_chunks, C_HEADS // 2, PAIR, PAIR), F32),
                        pltpu.VMEM((n_chunks, C_HEADS // 2, PAIR, 1), F32)],
        compiler_params=_params(("parallel", "arbitrary")), name="gdn",
    )(zc, scal, buf0, s0, cw, alog, dt, ng, ones_hd)


def _outproj_kernel(h_ref, ya_ref, yb_ref, yc_ref, wo_ref, g_ref, wq_ref, hn_ref, x2_ref, q_ref):
    mix = jnp.concatenate([ya_ref[...], yb_ref[...], yc_ref[...]], axis=1)
    h = h_ref[...] + _dot(mix, wo_ref[...])
    hn_ref[...] = h
    x2 = _rms(h, g_ref[...]).astype(BF16)
    x2_ref[...] = x2
    q = _dot(x2, wq_ref[...])
    for hp in range(N_SUBQ):
        q_ref[hp] = q[:, hp * P_DK:(hp + 1) * P_DK]


N_SUBQ = 2 * P_HEADS


def out_proj(h, ya, yb, yc, wo, g, wq, tm):
    n = h.shape[0]
    row = lambda w: pl.BlockSpec((tm, w), lambda i: (i, 0))
    full = lambda a: pl.BlockSpec(a.shape, lambda i: (0,) * a.ndim)
    return pl.pallas_call(
        _outproj_kernel, grid=(n // tm,),
        out_shape=[jax.ShapeDtypeStruct((n, D_MODEL), F32), jax.ShapeDtypeStruct((n, D_MODEL), BF16),
                   jax.ShapeDtypeStruct((N_SUBQ, n, P_DK), F32)],
        in_specs=[row(D_MODEL), row(A_WIDTH), row(B_WIDTH), row(C_WIDTH), full(wo), full(g), full(wq)],
        out_specs=[row(D_MODEL), row(D_MODEL), pl.BlockSpec((N_SUBQ, tm, P_DK), lambda i: (0, i, 0))],
        compiler_params=_params(("parallel",)), name="out_proj",
    )(h, ya, yb, yc, wo, g, wq)


def _top16_rows(items):
    state = [s for s, _ in items]
    rows = [_iota(s.shape, 0) for s in state]
    vals = [[] for _ in items]
    pays = [[] for _ in items]
    for _ in range(P_TOPK):
        for k, (_, payload) in enumerate(items):
            s, row = state[k], rows[k]
            m = jnp.max(s, axis=0, keepdims=True)
            pos = jnp.min(jnp.where(s == m, row, s.shape[0]), axis=0, keepdims=True)
            hit = row == pos
            vals[k].append(m)
            pays[k].append(pos if payload is None else jnp.max(jnp.where(hit, payload, -1), axis=0, keepdims=True))
            state[k] = jnp.where(hit, -jnp.inf, s)
    return [(jnp.concatenate(v, axis=0), jnp.concatenate(p, axis=0)) for v, p in zip(vals, pays)]


def _retrieval(score_pairs):
    tops = _top16_rows([(s, None) for pair in score_pairs for s in pair])
    cands = []
    for u in range(len(score_pairs)):
        (s0, i0), (s1, i1) = tops[2 * u], tops[2 * u + 1]
        tn = s0.shape[1]
        cand = [s0[0:1, :] + s1]
        cidx = [i0[0:1, :] * N_KEYS + i1]
        b8 = _iota((8, tn), 0)
        for a in range(1, 8):
            va = s0[a:a + 1, :] + s1[0:8, :]
            nb = P_TOPK // (a + 1)
            cand.append(va if nb >= 8 else jnp.where(b8 < nb, va, -jnp.inf))
            cidx.append(i0[a:a + 1, :] * N_KEYS + i1[0:8, :])
        cand.append(s0[8:16, :] + s1[0:1, :])
        cidx.append(i0[8:16, :] * N_KEYS + i1[0:1, :])
        cands.append((jnp.concatenate(cand, axis=0), jnp.concatenate(cidx, axis=0)))
    out = []
    for best, e in _top16_rows(cands):
        w = jnp.exp(best - best[0:1, :])
        out.append((e, w / jnp.sum(w, axis=0, keepdims=True)))
    return out


def _peer_topk_kernel(q_ref, keys_ref, eidx_ref, gate_ref):
    eidx, gates = [], []
    for h0 in range(0, P_HEADS, 2):
        scores = [tuple(_dot_nt(keys_ref[h, half], q_ref[2 * h + half], HI) for half in range(2))
                  for h in (h0, h0 + 1)]
        for e, g in _retrieval(scores):
            eidx.append(e)
            gates.append(g)
    eidx_ref[...] = jnp.concatenate(eidx, axis=0).T
    gate_ref[...] = jnp.concatenate(gates, axis=0).T


def peer_topk(q, keys, tn):
    n = q.shape[1]
    hk = P_HEADS * P_TOPK
    return pl.pallas_call(
        _peer_topk_kernel, grid=(n // tn,),
        out_shape=[jax.ShapeDtypeStruct((n, hk), I32), jax.ShapeDtypeStruct((n, hk), F32)],
        in_specs=[pl.BlockSpec((N_SUBQ, tn, P_DK), lambda i: (0, i, 0)), pl.BlockSpec(keys.shape, lambda i: (0, 0, 0, 0))],
        out_specs=[pl.BlockSpec((tn, hk), lambda i: (i, 0))] * 2,
        compiler_params=_params(("parallel",)), name="peer_topk",
    )(q, keys)


E_CHUNK = 1024
SLABS = E_CHUNK // N_KEYS
KEY_BITS = N_KEYS.bit_length() - 1


N_CHUNKS = N_EXPERTS // E_CHUNK
N_PAIRS = N_CHUNKS // 2
HALF_KEYS = N_KEYS // 2
TOKEN_UNROLL = 8
U32 = jnp.uint32
HI_HALF = 0xFFFF0000


def _expert_scores(step, x_ref, ut_ref, eidx_ref, pf_ref):
    e = eidx_ref[...]
    ii = jnp.right_shift(e, KEY_BITS)
    jj = jnp.bitwise_and(e, N_KEYS - 1)
    pc = _dot(x_ref[...], ut_ref[...])
    pf = pf_ref[...]
    for s in range(SLABS):
        got = jnp.take_along_axis(pc[:, s * N_KEYS:(s + 1) * N_KEYS], jj, axis=1)
        pf = jnp.where(ii == step * SLABS + s, got, pf)
    pf_ref[...] = pf


def _expert_hidden(eidx_ref, gate_ref, pf_ref, hm_ref):
    tn, hk = pf_ref.shape
    pf_ref[...] = _gelu_erf(pf_ref[...]) * gate_ref[...]
    sub = _iota((N_KEYS, hk), 0)

    def tokens(t, carry):
        for r in range(TOKEN_UNROLL):
            n = t * TOKEN_UNROLL + r
            e = eidx_ref[pl.ds(n, 1), :]
            hrow = pf_ref[pl.ds(n, 1), :]
            w = jnp.where(sub == jnp.right_shift(e, KEY_BITS), hrow, 0.0).astype(BF16)
            bt = jnp.where(sub == jnp.bitwise_and(e, N_KEYS - 1), 1.0, 0.0).astype(BF16)
            hn = _dot_nt(w, bt).astype(BF16).astype(F32)
            hi = pltpu.bitcast(hn[:HALF_KEYS, :], U32)
            lo = lax.shift_right_logical(pltpu.bitcast(hn[HALF_KEYS:, :], U32), U32(16))
            hm_ref[pl.ds(pl.multiple_of(n * HALF_KEYS, HALF_KEYS), HALF_KEYS), :] = hi | lo
        return carry

    lax.fori_loop(0, tn // TOKEN_UNROLL, tokens, 0)


def _expert_values(cp, hm_ref, vlo_ref, vhi_ref, o_ref):
    tn = o_ref.shape[0]
    words = [hm_ref[pl.ds(cp * SLABS + s, tn, stride=HALF_KEYS), :] for s in range(SLABS)]
    hi = [pltpu.bitcast(wd & U32(HI_HALF), F32).astype(BF16) for wd in words]
    lo = [pltpu.bitcast(lax.shift_left(wd, U32(16)), F32).astype(BF16) for wd in words]
    o_ref[...] += (_dot(jnp.concatenate(hi, axis=1), vlo_ref[...])
                   + _dot(jnp.concatenate(lo, axis=1), vhi_ref[...]))


def _peer_expert_kernel(x_ref, h_ref, eidx_ref, gate_ref, ut_ref, vlo_ref, vhi_ref, o_ref, pf_ref, hm_ref):
    step = pl.program_id(1)

    @pl.when(step == 0)
    def _():
        pf_ref[...] = jnp.zeros(pf_ref.shape, F32)

    @pl.when(step < N_CHUNKS)
    def _():
        _expert_scores(step, x_ref, ut_ref, eidx_ref, pf_ref)

    @pl.when(step == N_CHUNKS - 1)
    def _():
        _expert_hidden(eidx_ref, gate_ref, pf_ref, hm_ref)
        o_ref[...] = h_ref[...]

    @pl.when(step >= N_CHUNKS)
    def _():
        _expert_values(step - N_CHUNKS, hm_ref, vlo_ref, vhi_ref, o_ref)


SUBTILE = 128


def _peer_fused_kernel(x_ref, h_ref, q_ref, keys_ref, ut_ref, vlo_ref, vhi_ref, o_ref,
                       pf_ref, hm_ref, eidx_ref, gate_ref, et_ref, gt_ref):
    tile = pl.program_id(0)
    step = pl.program_id(1)
    hk = P_HEADS * P_TOPK
    n_sub = et_ref.shape[1]
    slot = tile % 2
    first = tile == 0
    later = tile > 0

    @pl.when((step == 0) & later)
    def _():
        for sub in range(n_sub):
            rows = slice(sub * SUBTILE, (sub + 1) * SUBTILE)
            eidx_ref[rows, :] = et_ref[1 - slot, sub].reshape(hk, SUBTILE).T
            gate_ref[rows, :] = gt_ref[1 - slot, sub].reshape(hk, SUBTILE).T
        pf_ref[...] = jnp.zeros(pf_ref.shape, F32)

    def retrieve(units):
        where, scores = [], []
        for u in units:
            head = u % P_HEADS
            sub = u // P_HEADS
            rows = pl.ds(pl.multiple_of(sub * SUBTILE, SUBTILE), SUBTILE)
            where.append((sub, head))
            scores.append(tuple(_dot_nt(keys_ref[head, half], q_ref[2 * head + half, rows, :], HI)
                                for half in range(2)))
        for (sub, head), (e, g) in zip(where, _retrieval(scores)):
            et_ref[slot, sub, head] = e
            gt_ref[slot, sub, head] = g

    @pl.when((step < N_CHUNKS) & first)
    def _():
        retrieve([step])

    @pl.when((step < N_CHUNKS) & later)
    def _():
        retrieve([step])
        _expert_scores(step, x_ref, ut_ref, eidx_ref, pf_ref)

    @pl.when((step == N_CHUNKS - 1) & later)
    def _():
        _expert_hidden(eidx_ref, gate_ref, pf_ref, hm_ref)
        o_ref[...] = h_ref[...]

    @pl.when((step >= N_CHUNKS) & first)
    def _():
        cp = step - N_CHUNKS
        retrieve([N_CHUNKS + 2 * cp, N_CHUNKS + 2 * cp + 1])

    @pl.when((step >= N_CHUNKS) & later)
    def _():
        cp = step - N_CHUNKS
        retrieve([N_CHUNKS + 2 * cp, N_CHUNKS + 2 * cp + 1])
        _expert_values(cp, hm_ref, vlo_ref, vhi_ref, o_ref)


def peer_fused(x2, h, q, keys, ut, v, tn):
    n = x2.shape[0]
    nt = n // tn
    hk = P_HEADS * P_TOPK
    n_sub = tn // SUBTILE
    assert n_sub * P_HEADS == N_CHUNKS + 2 * N_PAIRS
    prev = lambda w: pl.BlockSpec((tn, w), lambda i, s: (jnp.maximum(i - 1, 0), 0))
    pair = lambda s: jnp.clip(s - N_CHUNKS, 0, N_PAIRS - 1)
    return pl.pallas_call(
        _peer_fused_kernel, grid=(nt + 1, N_CHUNKS + N_PAIRS),
        out_shape=jax.ShapeDtypeStruct((n, D_MODEL), F32),
        in_specs=[prev(D_MODEL), prev(D_MODEL),
                  pl.BlockSpec((N_SUBQ, tn, P_DK), lambda i, s: (0, jnp.minimum(i, nt - 1), 0)),
                  pl.BlockSpec(keys.shape, lambda i, s: (0, 0, 0, 0)),
                  pl.BlockSpec((D_MODEL, E_CHUNK), lambda i, s: (0, jnp.minimum(s, N_CHUNKS - 1))),
                  pl.BlockSpec((E_CHUNK, D_MODEL), lambda i, s: (pair(s), 0)),
                  pl.BlockSpec((E_CHUNK, D_MODEL), lambda i, s: (pair(s) + N_PAIRS, 0))],
        out_specs=prev(D_MODEL),
        scratch_shapes=[pltpu.VMEM((tn, hk), F32), pltpu.VMEM((tn * HALF_KEYS, N_KEYS), U32),
                        pltpu.VMEM((tn, hk), I32), pltpu.VMEM((tn, hk), F32),
                        pltpu.VMEM((2, n_sub, P_HEADS, P_TOPK, SUBTILE), I32),
                        pltpu.VMEM((2, n_sub, P_HEADS, P_TOPK, SUBTILE), F32)],
        compiler_params=_params(("arbitrary", "arbitrary"), 58 * 1024 * 1024), name="peer_fused",
    )(x2, h, q, keys, ut, v, v)


def peer_experts(x2, h, eidx, gate, ut, v, tn):
    n = x2.shape[0]
    hk = P_HEADS * P_TOPK
    row = lambda w: pl.BlockSpec((tn, w), lambda i, s: (i, 0))
    pair = lambda s: jnp.clip(s - N_CHUNKS, 0, N_PAIRS - 1)
    return pl.pallas_call(
        _peer_expert_kernel, grid=(n // tn, N_CHUNKS + N_PAIRS),
        out_shape=jax.ShapeDtypeStruct((n, D_MODEL), F32),
        in_specs=[row(D_MODEL), row(D_MODEL), row(hk), row(hk),
                  pl.BlockSpec((D_MODEL, E_CHUNK), lambda i, s: (0, jnp.minimum(s, N_CHUNKS - 1))),
                  pl.BlockSpec((E_CHUNK, D_MODEL), lambda i, s: (pair(s), 0)),
                  pl.BlockSpec((E_CHUNK, D_MODEL), lambda i, s: (pair(s) + N_PAIRS, 0))],
        out_specs=row(D_MODEL),
        scratch_shapes=[pltpu.VMEM((tn, hk), F32), pltpu.VMEM((tn * HALF_KEYS, N_KEYS), U32)],
        compiler_params=_params(("parallel", "arbitrary"), 56 * 1024 * 1024), name="peer_experts",
    )(x2, h, eidx, gate, ut, v, v)


def _final_norm_kernel(h_ref, g_ref, o_ref):
    o_ref[...] = _rms(h_ref[...], g_ref[...])


def final_norm(h, g, tm):
    n = h.shape[0]
    row = pl.BlockSpec((tm, D_MODEL), lambda i: (i, 0))
    return pl.pallas_call(
        _final_norm_kernel, grid=(n // tm,), out_shape=jax.ShapeDtypeStruct(h.shape, F32),
        in_specs=[row, pl.BlockSpec(g.shape, lambda i: (0, 0))], out_specs=row,
        compiler_params=_params(("parallel",)), name="final_norm",
    )(h, g)


def _block_diag(w):
    g, a, b = w.shape
    eye = jnp.eye(g, dtype=w.dtype)
    return (eye[:, None, :, None] * w[:, :, None, :]).reshape(g * a, g * b)


def _prep_layer(W, l):
    w_in = W["w_in"][l]
    return dict(
        g1=W["norm1_g"][l][None, :],
        wa=w_in[:, OFF_AQ:OFF_AF].astype(BF16),
        wb=w_in[:, OFF_BX:OFF_CQKV].astype(BF16),
        wc=jnp.concatenate([w_in[:, OFF_CQKV:OFF_CA], w_in[:, OFF_CG:D_IN]], axis=1).astype(BF16),
        ws=jnp.concatenate([w_in[:, OFF_AF:OFF_BX], w_in[:, OFF_CA:OFF_CG]], axis=1).T,
        bf=W["b_f"][l][:, None],
        lru_cw=W["lru_conv_w"][l], lru_cb=W["lru_conv_b"][l][None, :],
        lru_wa=_block_diag(W["lru_wa"][l]), lru_ba=W["lru_ba"][l][None, :],
        lru_wx=_block_diag(W["lru_wx"][l]), lru_bx=W["lru_bx"][l][None, :],
        lru_lam=W["lru_lambda"][l][None, :],
        gdn_cw=W["gdn_conv_w"][l], gdn_alog=W["gdn_A_log"][l][:, None], gdn_dt=W["gdn_dt_bias"][l][:, None],
        gdn_ng=jnp.tile(W["gdn_norm_g"][l], C_HEADS)[None, :],
        wo=W["w_out"][l].astype(BF16), g2=W["norm2_g"][l][None, :], wq=W["peer_wq"][l].astype(BF16),
        keys=W["peer_keys"][l], ut=W["peer_u"][l].T.astype(BF16), v=W["peer_v"][l].astype(BF16),
    )


def _pair_state(s):
    nb = s.shape[0]
    s = s.reshape(nb, C_HEADS // 2, 2, C_HEAD_DIM, C_HEAD_DIM)
    eye = jnp.eye(2, dtype=s.dtype)
    return (s[:, :, :, :, None, :] * eye[None, None, :, None, :, None]).reshape(nb, C_HEADS // 2, PAIR, PAIR)


def _unpair_state(s):
    nb = s.shape[0]
    s = s.reshape(nb, C_HEADS // 2, 2, C_HEAD_DIM, 2, C_HEAD_DIM)
    return jnp.stack([s[:, :, 0, :, 0, :], s[:, :, 1, :, 1, :]], axis=2).reshape(nb, C_HEADS, C_HEAD_DIM, C_HEAD_DIM)


def _trunk(x, nb, t_real, paged, lru_h, lru_buf, gdn_s, gdn_buf, layers, final_g, cfg):
    t_rows = x.shape[1]
    n = nb * t_rows
    h = x.reshape(n, D_MODEL)
    ones_hd = _block_diag(jnp.full((C_HEADS, C_HEAD_DIM, C_HEAD_DIM), 1.0 / C_HEAD_DIM, F32))
    outs = []
    for l, lw in enumerate(layers):
        q, k, v, kb, vb, zb, zc, st = in_proj(h, lw["g1"], lw["wa"], lw["wb"], lw["wc"], lw["ws"], cfg["tm"])
        if paged is None:
            lf, c = logf_rows(st, lw["bf"], nb, True)
            ya = attn_prompt(q, kb, vb, c.reshape(A_HEADS // 2, 2, n), nb, cfg["tq"])
            n_chunks = cfg["tc_gdn"] // GDN_CHUNK
            scal = st.reshape(N_SMALL, n // GDN_CHUNK, GDN_CHUNK).transpose(1, 0, 2)
        else:
            ck, cv, clt, pt = paged
            lf, _ = logf_rows(st, lw["bf"], 1, False)
            lfn = lf.reshape(A_HEADS, nb, t_rows).transpose(1, 0, 2)
            lfn = jnp.where(jnp.arange(t_rows) < t_real, lfn, 0.0)
            lfn = jnp.pad(lfn, ((0, 0), (0, 0), (0, PAGE_SIZE - t_rows)))
            ya = attn_sample(pt, q, ck, cv, clt, k, v, lfn, l, t_real)
            n_chunks = 1
            scal = jnp.pad(st.reshape(N_SMALL, nb, t_rows).transpose(1, 0, 2),
                           ((0, 0), (0, 0), (0, GDN_CHUNK - t_rows)))
        t_valid = min(t_real, cfg["tc_lru"])
        yb, lru_ht, lru_nb = lru_mixer(zb, lru_buf[l], lru_h[l][:, None, :], lw["lru_cw"], lw["lru_cb"],
                                       lw["lru_wa"], lw["lru_ba"], lw["lru_wx"], lw["lru_bx"], lw["lru_lam"],
                                       nb, cfg["tc_lru"], t_valid)
        t_valid = min(t_real, cfg["tc_gdn"])
        yc, gdn_st, gdn_nb = gdn_mixer(zc, scal, gdn_buf[l], _pair_state(gdn_s[l]), lw["gdn_cw"], lw["gdn_alog"],
                                       lw["gdn_dt"], lw["gdn_ng"], ones_hd, nb, cfg["tc_gdn"], t_valid, n_chunks)
        hmid, x2, qp = out_proj(h, ya, yb, yc, lw["wo"], lw["g2"], lw["wq"], cfg["tm"])
        if cfg["fused_peer"]:
            h = peer_fused(x2, hmid, qp, lw["keys"], lw["ut"], lw["v"], cfg["tn_peer"])
        else:
            eidx, gate = peer_topk(qp, lw["keys"], cfg["tn_topk"])
            h = peer_experts(x2, hmid, eidx, gate, lw["ut"], lw["v"], cfg["tn_peer"])
        outs.append((k.reshape(nb, t_rows, A_HEADS, A_HEAD_DIM)[:, :t_real],
                     v.reshape(nb, t_rows, A_HEADS, A_HEAD_DIM)[:, :t_real],
                     lf.reshape(A_HEADS, nb, t_rows).transpose(1, 2, 0)[:, :t_real],
                     lru_ht[:, 0, :], lru_nb, _unpair_state(gdn_st), gdn_nb))
    y = final_norm(h, final_g[None, :], cfg["tm"]).reshape(nb, t_rows, D_MODEL)[:, :t_real]
    return y, tuple(jnp.stack([o[i] for o in outs], axis=0) for i in range(len(outs[0])))


PROMPT_CFG = dict(tm=512, tq=1024, tc_lru=512, tc_gdn=256, tn_topk=None, tn_peer=512, fused_peer=True)
SAMPLE_CFG = dict(tm=256, tq=None, tc_lru=SAMPLE_ROWS, tc_gdn=SAMPLE_ROWS, tn_topk=128, tn_peer=256, fused_peer=False)


def kernel(x_prompt, x_sample, cache_k, cache_v, cache_logf, page_table, state_lru_h, state_lru_conv,
           state_gdn_S, state_gdn_conv, norm1_g, w_in, b_f, lru_conv_w, lru_conv_b, lru_wa, lru_ba,
           lru_wx, lru_bx, lru_lambda, gdn_conv_w, gdn_A_log, gdn_dt_bias, gdn_norm_g, w_out, norm2_g,
           peer_wq, peer_keys, peer_u, peer_v, final_norm_g):
    W = dict(norm1_g=norm1_g, w_in=w_in, b_f=b_f, lru_conv_w=lru_conv_w, lru_conv_b=lru_conv_b, lru_wa=lru_wa,
             lru_ba=lru_ba, lru_wx=lru_wx, lru_bx=lru_bx, lru_lambda=lru_lambda, gdn_conv_w=gdn_conv_w,
             gdn_A_log=gdn_A_log, gdn_dt_bias=gdn_dt_bias, gdn_norm_g=gdn_norm_g, w_out=w_out, norm2_g=norm2_g,
             peer_wq=peer_wq, peer_keys=peer_keys, peer_u=peer_u, peer_v=peer_v)
    depth = w_in.shape[0]
    layers = [_prep_layer(W, l) for l in range(depth)]
    bp, tp = x_prompt.shape[:2]
    bs, ts = x_sample.shape[:2]

    ck = cache_k.transpose(0, 1, 3, 4, 2)
    cv = cache_v.transpose(0, 1, 3, 4, 2)
    clt = cache_logf.transpose(0, 1, 3, 2)
    xs = jnp.pad(x_sample, ((0, 0), (0, SAMPLE_ROWS - ts), (0, 0)))
    ys, ss = _trunk(xs, bs, ts, (ck, cv, clt, page_table), state_lru_h, state_lru_conv, state_gdn_S,
                    state_gdn_conv, layers, final_norm_g, SAMPLE_CFG)

    zeros = lambda *s: jnp.zeros((depth, bp) + s, F32)
    yp, sp = _trunk(x_prompt, bp, tp, None, zeros(B_WIDTH), zeros(CONV_W - 1, B_WIDTH),
                    zeros(C_HEADS, C_HEAD_DIM, C_HEAD_DIM), zeros(CONV_W - 1, 3 * C_WIDTH),
                    layers, final_norm_g, PROMPT_CFG)

    return (yp, ys, sp[0], sp[1], sp[2], ss[0], ss[1], ss[2], sp[3], ss[3], sp[4], ss[4], sp[5], ss[5], sp[6], ss[6])
```
